```python
import math
import jax, jax.numpy as jnp
from jax import lax
import numpy as np

D_MODEL = 2048
BATCH = 8
SEQ = 2048
DEPTH = 2
DEC_BATCH = 128
DEC_SEQ = 4
PAST_LEN = 2048
PAGE_SIZE = 128

HEAD_DIM = 128
N_FOX_HEADS = D_MODEL // (2 * HEAD_DIM)
FOX_WIDTH = N_FOX_HEADS * HEAD_DIM
N_DIFF_HEADS = D_MODEL // (4 * HEAD_DIM)
DIFF_QK_WIDTH = N_DIFF_HEADS * 2 * HEAD_DIM
DIFF_V_WIDTH = N_DIFF_HEADS * 2 * HEAD_DIM
N_BRANCHES = 2
D_IN = 3 * FOX_WIDTH + N_FOX_HEADS + 2 * DIFF_QK_WIDTH + DIFF_V_WIDTH + N_BRANCHES * D_MODEL
ROT_DIM = HEAD_DIM // 4
ROPE_THETA = 500000.0
D_FF = 4 * D_MODEL
Q_BLOCK = 128
ATTN_SCALE = HEAD_DIM ** -0.5
NORM_EPS = 1e-6
SUBLN_EPS = 1e-5
FORGET_BIAS_INIT = 3.0

kernel_name = "fox_diffattn_gated_hybrid_step"


def rmsnorm(x, g, eps):
    xf = x.astype(jnp.float32)
    y = xf * lax.rsqrt(jnp.mean(xf * xf, axis=-1, keepdims=True) + eps) * g.astype(jnp.float32)
    return y.astype(x.dtype)


def rope_partial(x, pos):
    half = ROT_DIM // 2
    inv = ROPE_THETA ** (-jnp.arange(half, dtype=jnp.float32) / half)
    ang = pos.astype(jnp.float32)[:, None] * inv[None, :]
    cos = jnp.cos(ang)[:, None, :]
    sin = jnp.sin(ang)[:, None, :]
    xr = x[..., :ROT_DIM].astype(jnp.float32)
    x1, x2 = xr[..., :half], xr[..., half:]
    rot = jnp.concatenate([x1 * cos - x2 * sin, x2 * cos + x1 * sin], axis=-1).astype(x.dtype)
    return jnp.concatenate([rot, x[..., ROT_DIM:]], axis=-1)


def masked_softmax(s, mask):
    return jax.nn.softmax(jnp.where(mask, s, -jnp.inf), axis=-1)


def two_part_softmax(s_past, s_new):
    P = s_past.shape[-1]
    T = s_new.shape[-1]
    s_new = jnp.where(jnp.tri(T, dtype=bool), s_new, -jnp.inf)
    p = jax.nn.softmax(jnp.concatenate([s_past, s_new], axis=-1), axis=-1)
    return p[..., :P], p[..., P:]


def project(x, pos, g_attn, w_in, b_f):
    B, S, _ = x.shape
    h = rmsnorm(x, g_attn, NORM_EPS)
    z = h @ w_in
    sizes = (FOX_WIDTH, FOX_WIDTH, FOX_WIDTH, N_FOX_HEADS,
             DIFF_QK_WIDTH, DIFF_QK_WIDTH, DIFF_V_WIDTH, N_BRANCHES * D_MODEL)
    points = [sum(sizes[:i + 1]) for i in range(len(sizes) - 1)]
    zq, zk, zv, zf, zdq, zdk, zdv, zg = jnp.split(z, points, axis=-1)
    fq = zq.reshape(B, S, N_FOX_HEADS, HEAD_DIM)
    fk = zk.reshape(B, S, N_FOX_HEADS, HEAD_DIM)
    fv = zv.reshape(B, S, N_FOX_HEADS, HEAD_DIM)
    logf = jax.nn.log_sigmoid(zf.astype(jnp.float32) + b_f.astype(jnp.float32))
    dq = rope_partial(zdq.reshape(B, S, 2 * N_DIFF_HEADS, HEAD_DIM), pos).reshape(B, S, N_DIFF_HEADS, 2, HEAD_DIM)
    dk = rope_partial(zdk.reshape(B, S, 2 * N_DIFF_HEADS, HEAD_DIM), pos).reshape(B, S, N_DIFF_HEADS, 2, HEAD_DIM)
    dv = zdv.reshape(B, S, N_DIFF_HEADS, 2 * HEAD_DIM)
    gates = jax.nn.sigmoid(zg.astype(jnp.float32)).reshape(B, S, N_BRANCHES, D_MODEL).astype(x.dtype)
    return fq, fk, fv, logf, dq, dk, dv, gates


def fox_attend_prompt(q, k, v, logf):
    B, S, H, D = q.shape
    nb = S // Q_BLOCK
    F_k = jnp.cumsum(logf.astype(jnp.float32), axis=1).transpose(0, 2, 1)
    kpos = jnp.arange(S)
    qb = jnp.moveaxis(q.reshape(B, nb, Q_BLOCK, H, D), 1, 0)
    Fb = jnp.moveaxis(F_k.reshape(B, H, nb, Q_BLOCK), 2, 0)
    qpos = jnp.arange(S).reshape(nb, Q_BLOCK)

    def block(args):
        qi, Fi, pi = args
        s = jnp.einsum('bqhd,bkhd->bhqk', qi, k).astype(jnp.float32) * ATTN_SCALE
        s = s + Fi[..., :, None] - F_k[:, :, None, :]
        p = masked_softmax(s, pi[:, None] >= kpos[None, :])
        return jnp.einsum('bhqk,bkhd->bqhd', p.astype(v.dtype), v)

    o = lax.map(block, (qb, Fb, qpos))
    return jnp.moveaxis(o, 0, 1).reshape(B, S, H, D)


def fox_attend_sample(q, k, v, logf, k_past, v_past, logf_past):
    P = k_past.shape[1]
    F = jnp.cumsum(jnp.concatenate([logf_past.astype(jnp.float32), logf.astype(jnp.float32)], axis=1),
                   axis=1).transpose(0, 2, 1)
    Fq = F[..., P:]
    s_past = jnp.einsum('bthd,bkhd->bhtk', q, k_past).astype(jnp.float32) * ATTN_SCALE
    s_past = s_past + Fq[..., :, None] - F[:, :, None, :P]
    s_new = jnp.einsum('bthd,bkhd->bhtk', q, k).astype(jnp.float32) * ATTN_SCALE
    s_new = s_new + Fq[..., :, None] - Fq[:, :, None, :]
    p_past, p_new = two_part_softmax(s_past, s_new)
    return (jnp.einsum('bhtk,bkhd->bthd', p_past.astype(v.dtype), v_past)
            + jnp.einsum('bhtk,bkhd->bthd', p_new.astype(v.dtype), v))


def diff_attend_prompt(q, k, v, lam):
    B, S, H, _, D = q.shape
    nb = S // Q_BLOCK
    kpos = jnp.arange(S)
    qb = jnp.moveaxis(q.reshape(B, nb, Q_BLOCK, H, 2, D), 1, 0)
    qpos = jnp.arange(S).reshape(nb, Q_BLOCK)

    def block(args):
        qi, pi = args
        s = jnp.einsum('bqhcd,bkhcd->bhcqk', qi, k).astype(jnp.float32) * ATTN_SCALE
        p = masked_softmax(s, pi[:, None] >= kpos[None, :])
        p = p[:, :, 0] - lam * p[:, :, 1]
        return jnp.einsum('bhqk,bkhe->bqhe', p.astype(v.dtype), v)

    o = lax.map(block, (qb, qpos))
    return jnp.moveaxis(o, 0, 1).reshape(B, S, H, 2 * D)


def diff_attend_sample(q, k, v, lam, k_past, v_past):
    s_past = jnp.einsum('bthcd,bkhcd->bhctk', q, k_past).astype(jnp.float32) * ATTN_SCALE
    s_new = jnp.einsum('bthcd,bkhcd->bhctk', q, k).astype(jnp.float32) * ATTN_SCALE
    p_past, p_new = two_part_softmax(s_past, s_new)
    p_past = p_past[:, :, 0] - lam * p_past[:, :, 1]
    p_new = p_new[:, :, 0] - lam * p_new[:, :, 1]
    return (jnp.einsum('bhtk,bkhe->bthe', p_past.astype(v.dtype), v_past)
            + jnp.einsum('bhtk,bkhe->bthe', p_new.astype(v.dtype), v))


def finish(x, fox_o, diff_o, gates, lam_init, g_subln, w_fox_out, w_diff_out, w_o, g_mlp, w_up, w_down):
    B, S, _ = x.shape
    diff_o = rmsnorm(diff_o, g_subln, SUBLN_EPS) * (1.0 - lam_init)
    a = fox_o.reshape(B, S, FOX_WIDTH) @ w_fox_out
    b = diff_o.reshape(B, S, DIFF_V_WIDTH) @ w_diff_out
    x = x + (gates[..., 0, :] * a + gates[..., 1, :] * b) @ w_o
    h = rmsnorm(x, g_mlp, NORM_EPS)
    return x + jnp.square(jax.nn.relu(h @ w_up)) @ w_down


def setup_inputs(seed: int = 0) -> dict:
    key = jax.random.key(seed)
    ks = jax.random.split(key, 24)
    n_pages = PAST_LEN // PAGE_SIZE
    n_used = DEC_BATCH * n_pages
    n_pool = n_used + max(1, n_used // 4)
    nrm = jax.random.normal
    page_table = jax.random.permutation(ks[0], n_pool)[:n_used].reshape(DEC_BATCH, n_pages).astype(jnp.int32)
    return {
        "x_prompt": nrm(ks[1], (BATCH, SEQ, D_MODEL), jnp.float32),
        "x_sample": nrm(ks[2], (DEC_BATCH, DEC_SEQ, D_MODEL), jnp.float32),
        "cache_fox_k": nrm(ks[3], (DEPTH, n_pool, PAGE_SIZE, N_FOX_HEADS, HEAD_DIM), jnp.float32),
        "cache_fox_v": nrm(ks[4], (DEPTH, n_pool, PAGE_SIZE, N_FOX_HEADS, HEAD_DIM), jnp.float32),
        "cache_fox_logf": jax.nn.log_sigmoid(FORGET_BIAS_INIT + nrm(ks[5], (DEPTH, n_pool, PAGE_SIZE, N_FOX_HEADS), jnp.float32)),
        "cache_diff_k": nrm(ks[6], (DEPTH, n_pool, PAGE_SIZE, N_DIFF_HEADS, 2, HEAD_DIM), jnp.float32),
        "cache_diff_v": nrm(ks[7], (DEPTH, n_pool, PAGE_SIZE, N_DIFF_HEADS, 2 * HEAD_DIM), jnp.float32),
        "page_table": page_table,
        "g_attn": 1.0 + 0.01 * nrm(ks[8], (DEPTH, D_MODEL), jnp.float32),
        "w_in": nrm(ks[9], (DEPTH, D_MODEL, D_IN), jnp.float32) * D_MODEL ** -0.5,
        "b_f": FORGET_BIAS_INIT + 0.1 * nrm(ks[10], (DEPTH, N_FOX_HEADS), jnp.float32),
        "lam_q1": 0.1 * nrm(ks[11], (DEPTH, HEAD_DIM), jnp.float32),
        "lam_k1": 0.1 * nrm(ks[12], (DEPTH, HEAD_DIM), jnp.float32),
        "lam_q2": 0.1 * nrm(ks[13], (DEPTH, HEAD_DIM), jnp.float32),
        "lam_k2": 0.1 * nrm(ks[14], (DEPTH, HEAD_DIM), jnp.float32),
        "g_subln": 1.0 + 0.01 * nrm(ks[15], (DEPTH, 2 * HEAD_DIM), jnp.float32),
        "w_fox_out": nrm(ks[16], (DEPTH, FOX_WIDTH, D_MODEL), jnp.float32) * FOX_WIDTH ** -0.5,
        "w_diff_out": nrm(ks[17], (DEPTH, DIFF_V_WIDTH, D_MODEL), jnp.float32) * DIFF_V_WIDTH ** -0.5,
        "w_o": nrm(ks[18], (DEPTH, D_MODEL, D_MODEL), jnp.float32) * D_MODEL ** -0.5,
        "g_mlp": 1.0 + 0.01 * nrm(ks[19], (DEPTH, D_MODEL), jnp.float32),
        "w_up": nrm(ks[20], (DEPTH, D_MODEL, D_FF), jnp.float32) * D_MODEL ** -0.5,
        "w_down": nrm(ks[21], (DEPTH, D_FF, D_MODEL), jnp.float32) * D_FF ** -0.5,
        "g_final": 1.0 + 0.01 * nrm(ks[22], (D_MODEL,), jnp.float32),
    }


def reference(x_prompt, x_sample, cache_fox_k, cache_fox_v, cache_fox_logf, cache_diff_k, cache_diff_v,
              page_table, g_attn, w_in, b_f, lam_q1, lam_k1, lam_q2, lam_k2, g_subln,
              w_fox_out, w_diff_out, w_o, g_mlp, w_up, w_down, g_final):
    dec_b, n_pages = page_table.shape
    past_len = n_pages * cache_fox_k.shape[2]
    pos_p = jnp.arange(x_prompt.shape[1])
    pos_s = past_len + jnp.arange(x_sample.shape[1])

    def gather(cache, l):
        g = cache[l, page_table]
        return g.reshape((dec_b, past_len) + g.shape[3:])

    xp, xs = x_prompt, x_sample
    fk_p, fv_p, flf_p, dk_p, dv_p = [], [], [], [], []
    fk_s, fv_s, flf_s, dk_s, dv_s = [], [], [], [], []
    for l in range(DEPTH):
        lam_init = 0.8 - 0.6 * math.exp(-0.3 * l)
        lam = (jnp.exp(jnp.sum(lam_q1[l].astype(jnp.float32) * lam_k1[l].astype(jnp.float32)))
               - jnp.exp(jnp.sum(lam_q2[l].astype(jnp.float32) * lam_k2[l].astype(jnp.float32)))
               + lam_init)
        fq, fk, fv, flf, dq, dk, dv, gt = project(xp, pos_p, g_attn[l], w_in[l], b_f[l])
        fo = fox_attend_prompt(fq, fk, fv, flf)
        do = diff_attend_prompt(dq, dk, dv, lam)
        xp = finish(xp, fo, do, gt, lam_init, g_subln[l], w_fox_out[l], w_diff_out[l], w_o[l],
                    g_mlp[l], w_up[l], w_down[l])
        fk_p.append(fk); fv_p.append(fv); flf_p.append(flf); dk_p.append(dk); dv_p.append(dv)
        fq, fk, fv, flf, dq, dk, dv, gt = project(xs, pos_s, g_attn[l], w_in[l], b_f[l])
        fo = fox_attend_sample(fq, fk, fv, flf, gather(cache_fox_k, l), gather(cache_fox_v, l),
                               gather(cache_fox_logf, l))
        do = diff_attend_sample(dq, dk, dv, lam, gather(cache_diff_k, l), gather(cache_diff_v, l))
        xs = finish(xs, fo, do, gt, lam_init, g_subln[l], w_fox_out[l], w_diff_out[l], w_o[l],
                    g_mlp[l], w_up[l], w_down[l])
        fk_s.append(fk); fv_s.append(fv); flf_s.append(flf); dk_s.append(dk); dv_s.append(dv)

    y_prompt = rmsnorm(xp, g_final, NORM_EPS)
    y_sample = rmsnorm(xs, g_final, NORM_EPS)
    return (y_prompt, y_sample,
            jnp.stack(fk_p), jnp.stack(fv_p), jnp.stack(flf_p), jnp.stack(dk_p), jnp.stack(dv_p),
            jnp.stack(fk_s), jnp.stack(fv_s), jnp.stack(flf_s), jnp.stack(dk_s), jnp.stack(dv_s))
```

```python
import functools
import math

import jax
import jax.numpy as jnp
from jax import lax
from jax.experimental import pallas as pl
from jax.experimental.pallas import tpu as pltpu

HEAD_DIM = 128
ROT_DIM = HEAD_DIM // 4
ROPE_THETA = 500000.0
ATTN_SCALE = HEAD_DIM ** -0.5
NORM_EPS = 1e-6
SUBLN_EPS = 1e-5
LANES = 128
GROUPS = 8
VMEM_LIMIT_BYTES = 56 * 1024 * 1024
PAGES_PER_STEP = 4
Q_ROWS = 16

_NT = (((1,), (1,)), ((), ()))

_ROWS_IDENTITY = tuple(range(GROUPS))
_ROWS_DIFF_V = tuple((j % 2) * 4 + j // 2 for j in range(GROUPS))


def _cparams(n_axes):
    return pltpu.CompilerParams(dimension_semantics=("arbitrary",) * n_axes,
                                vmem_limit_bytes=VMEM_LIMIT_BYTES)


def _rmsnorm_kernel(x_ref, g_ref, o_ref, *, eps):
    x = x_ref[...]
    ms = jnp.mean(x * x, axis=-1, keepdims=True)
    o_ref[...] = (x * lax.rsqrt(ms + eps) * g_ref[...]).astype(o_ref.dtype)


def rmsnorm(x, g, eps, out_dtype, name):
    m, d = x.shape
    bm = min(512, m)
    return pl.pallas_call(
        functools.partial(_rmsnorm_kernel, eps=eps),
        grid=(m // bm,),
        in_specs=[pl.BlockSpec((bm, d), lambda i: (i, 0)),
                  pl.BlockSpec((1, d), lambda i: (0, 0))],
        out_specs=pl.BlockSpec((bm, d), lambda i: (i, 0)),
        out_shape=jax.ShapeDtypeStruct((m, d), out_dtype),
        compiler_params=_cparams(1),
        name=name,
    )(x, g.reshape(1, d))


def _rope(acc, cos, sin_lo, sin_hi):
    half = ROT_DIM // 2
    chunks = []
    for c in range(acc.shape[1] // LANES):
        x = acc[:, c * LANES:(c + 1) * LANES]
        up = pltpu.roll(x, LANES - half, 1)
        dn = pltpu.roll(x, half, 1)
        chunks.append(x * cos + up * sin_lo + dn * sin_hi)
    return jnp.concatenate(chunks, axis=1)


def _mm_kernel(x_ref, w_ref, *refs, epilogue, scale, rope, w_transposed, head_rows):
    if w_transposed:
        acc = lax.dot_general(x_ref[...], w_ref[...], _NT, preferred_element_type=jnp.float32)
    else:
        acc = jnp.dot(x_ref[...], w_ref[...], preferred_element_type=jnp.float32)
    if scale != 1.0:
        acc = acc * scale
    if rope:
        acc = _rope(acc, refs[0][...], refs[1][...], refs[2][...])
        refs = refs[3:]
    if epilogue == "plain":
        for o_ref in refs:
            o_ref[...] = acc.astype(o_ref.dtype)
    elif epilogue == "heads":
        hm_ref, o_ref = refs
        bm = acc.shape[0]
        for j, r in enumerate(head_rows):
            hm_ref[pl.ds(r, bm, stride=GROUPS), :] = acc[:, j * LANES:(j + 1) * LANES]
        o_ref[...] = acc.astype(o_ref.dtype)
    elif epilogue == "sigmoid":
        (o_ref,) = refs
        o_ref[...] = jax.nn.sigmoid(acc).astype(o_ref.dtype)
    elif epilogue == "log_sigmoid":
        b_ref, o_ref = refs
        o_ref[...] = jax.nn.log_sigmoid(acc + b_ref[...]).astype(o_ref.dtype)
    elif epilogue == "relu2":
        (o_ref,) = refs
        r = jnp.maximum(acc, 0.0)
        o_ref[...] = (r * r).astype(o_ref.dtype)
    elif epilogue == "residual":
        res_ref, o_ref = refs
        o_ref[...] = res_ref[...] + acc
    elif epilogue == "residual_norm":
        res_ref, g_ref, x_ref_out, h_ref_out = refs
        xn = res_ref[...] + acc
        x_ref_out[...] = xn
        ms = jnp.mean(xn * xn, axis=-1, keepdims=True)
        h_ref_out[...] = (xn * lax.rsqrt(ms + NORM_EPS) * g_ref[...]).astype(h_ref_out.dtype)
    else:
        raise ValueError(epilogue)


def matmul(x, w, *, epilogue, out_dtypes, name, bm, bn, scale=1.0, extras=(), rope_tables=None,
           w_transposed=False, head_rows=None):
    m, k = x.shape
    n = w.shape[0] if w_transposed else w.shape[1]
    bm, bn = min(bm, m), min(bn, n)
    assert m % bm == 0 and n % bn == 0
    in_specs = [pl.BlockSpec((bm, k), lambda i, j: (i, 0)),
                pl.BlockSpec((bn, k), lambda i, j: (j, 0)) if w_transposed
                else pl.BlockSpec((k, bn), lambda i, j: (0, j))]
    args = [x, w]
    if rope_tables is not None:
        tables, n_blocks = rope_tables
        for t in tables:
            in_specs.append(pl.BlockSpec((bm, LANES), lambda i, j: (i % n_blocks, 0)))
            args.append(t)
    for arr, blk, imap in extras:
        in_specs.append(pl.BlockSpec(blk, imap))
        args.append(arr)
    if epilogue == "heads":
        assert bn == GROUPS * LANES == n
        out_specs = [pl.BlockSpec((bm * GROUPS, LANES), lambda i, j: (i, 0)),
                     pl.BlockSpec((bm, bn), lambda i, j: (i, j))]
        out_shape = [jax.ShapeDtypeStruct((m * GROUPS, LANES), out_dtypes[0]),
                     jax.ShapeDtypeStruct((m, n), out_dtypes[1])]
    else:
        out_specs = [pl.BlockSpec((bm, bn), lambda i, j: (i, j)) for _ in out_dtypes]
        out_shape = [jax.ShapeDtypeStruct((m, n), dt) for dt in out_dtypes]
    outs = pl.pallas_call(
        functools.partial(_mm_kernel, epilogue=epilogue, scale=scale, rope=rope_tables is not None,
                          w_transposed=w_transposed, head_rows=head_rows),
        grid=(m // bm, n // bn),
        in_specs=in_specs, out_specs=out_specs, out_shape=out_shape,
        compiler_params=_cparams(2),
        name=name,
    )(*args)
    return outs[0] if len(outs) == 1 else outs


def _mix_kernel(fo_ref, do_ref, wf_ref, wd_ref, g0_ref, g1_ref, o_ref):
    a = jnp.dot(fo_ref[...], wf_ref[...], preferred_element_type=jnp.float32)
    b = jnp.dot(do_ref[...], wd_ref[...], preferred_element_type=jnp.float32)
    o_ref[...] = (g0_ref[...] * a + g1_ref[...] * b).astype(o_ref.dtype)


def gated_mix(fo, do, w_fox_out, w_diff_out, gates, name):
    m, kf = fo.shape
    kd = do.shape[1]
    d = w_fox_out.shape[1]
    bm, bn = min(1024, m), 512
    nj = d // bn
    return pl.pallas_call(
        _mix_kernel,
        grid=(m // bm, nj),
        in_specs=[pl.BlockSpec((bm, kf), lambda i, j: (i, 0)),
                  pl.BlockSpec((bm, kd), lambda i, j: (i, 0)),
                  pl.BlockSpec((kf, bn), lambda i, j: (0, j)),
                  pl.BlockSpec((kd, bn), lambda i, j: (0, j)),
                  pl.BlockSpec((bm, bn), lambda i, j: (i, j)),
                  pl.BlockSpec((bm, bn), lambda i, j: (i, j + nj))],
        out_specs=pl.BlockSpec((bm, bn), lambda i, j: (i, j)),
        out_shape=jax.ShapeDtypeStruct((m, d), jnp.bfloat16),
        compiler_params=_cparams(2),
        name=name,
    )(fo, do, w_fox_out, w_diff_out, gates, gates)


def _lane_prefix(x):
    lane = lax.broadcasted_iota(jnp.int32, x.shape, 1)
    shift = 1
    while shift < LANES:
        x = x + jnp.where(lane >= shift, pltpu.roll(x, shift, 1), 0.0)
        shift *= 2
    return x


def _fox_prep_kernel(lf_ref, ft_ref, fn_ref, *, n_heads):
    seq = lf_ref.shape[1]
    lt = lf_ref[0].T[:n_heads, :]
    blocks = [_lane_prefix(lt[:, c * LANES:(c + 1) * LANES]) for c in range(seq // LANES)]
    carry = jnp.zeros((n_heads, 1), jnp.float32)
    outs = []
    for blk in blocks:
        blk = blk + carry
        carry = blk[:, LANES - 1:LANES]
        outs.append(blk)
    ft = jnp.concatenate(outs, axis=1)
    ft_ref[0] = ft
    padded = jnp.concatenate([ft, jnp.zeros((LANES - n_heads, seq), jnp.float32)], axis=0)
    fn_ref[0] = padded.T


def fox_prep(logf_pad, n_heads, name):
    b, seq, _ = logf_pad.shape
    return pl.pallas_call(
        functools.partial(_fox_prep_kernel, n_heads=n_heads),
        grid=(b,),
        in_specs=[pl.BlockSpec((1, seq, LANES), lambda i: (i, 0, 0))],
        out_specs=[pl.BlockSpec((1, n_heads, seq), lambda i: (i, 0, 0)),
                   pl.BlockSpec((1, seq, LANES), lambda i: (i, 0, 0))],
        out_shape=[jax.ShapeDtypeStruct((b, n_heads, seq), jnp.float32),
                   jax.ShapeDtypeStruct((b, seq, LANES), jnp.float32)],
        compiler_params=_cparams(1),
        name=name,
    )(logf_pad)


def _softmax_step(t, v, m_ref, l_ref, acc_ref, row_shift=None):
    t_max = jnp.max(t, axis=-1, keepdims=True)
    if row_shift is not None:
        t_max = t_max + row_shift
    m_prev = m_ref[...]
    m_new = jnp.maximum(m_prev, t_max)
    alpha = jnp.exp(m_prev - m_new)
    p = jnp.exp(t - (m_new - row_shift if row_shift is not None else m_new))
    l_ref[...] = alpha * l_ref[...] + jnp.sum(p, axis=-1, keepdims=True)
    acc_ref[...] = alpha * acc_ref[...] + jnp.dot(p.astype(v.dtype), v,
                                                  preferred_element_type=jnp.float32)
    m_ref[...] = m_new


def _causal_mask(tq, tk):
    return (lax.broadcasted_iota(jnp.int32, (tq, tk), 0)
            >= lax.broadcasted_iota(jnp.int32, (tq, tk), 1))


def _fox_attn_kernel(q_ref, k_ref, v_ref, ft_ref, fn_ref, o_ref, m_ref, l_ref, acc_ref, *, tq, n_heads):
    qi = pl.program_id(1)
    mask = _causal_mask(tq, tq)
    for h in range(n_heads):
        sl = slice(h * HEAD_DIM, (h + 1) * HEAD_DIM)
        q = q_ref[0, :, sl]
        fq = fn_ref[0, :, h:h + 1]
        m_ref[...] = jnp.full(m_ref.shape, -jnp.inf, jnp.float32)
        l_ref[...] = jnp.zeros(l_ref.shape, jnp.float32)
        acc_ref[...] = jnp.zeros(acc_ref.shape, jnp.float32)

        def logits(start, q=q, sl=sl, h=h):
            k = k_ref[0, pl.ds(start, tq), sl]
            fk = ft_ref[0, h:h + 1, pl.ds(start, tq)]
            s = lax.dot_general(q, k, _NT, preferred_element_type=jnp.float32)
            return s - fk, v_ref[0, pl.ds(start, tq), sl]

        def body(kj, carry, logits=logits, fq=fq):
            t, v = logits(pl.multiple_of(kj * tq, tq))
            _softmax_step(t, v, m_ref, l_ref, acc_ref, row_shift=fq)
            return carry

        lax.fori_loop(0, qi, body, 0)
        t, v = logits(pl.multiple_of(qi * tq, tq))
        _softmax_step(jnp.where(mask, t, -jnp.inf), v, m_ref, l_ref, acc_ref, row_shift=fq)
        o_ref[0, :, sl] = (acc_ref[...] / l_ref[...]).astype(o_ref.dtype)


def fox_attention(q, k, v, ft, fnat, n_heads, name, tq=512):
    b, seq, width = q.shape
    return pl.pallas_call(
        functools.partial(_fox_attn_kernel, tq=tq, n_heads=n_heads),
        grid=(b, seq // tq),
        in_specs=[pl.BlockSpec((1, tq, width), lambda i, j: (i, j, 0)),
                  pl.BlockSpec((1, seq, width), lambda i, j: (i, 0, 0)),
                  pl.BlockSpec((1, seq, width), lambda i, j: (i, 0, 0)),
                  pl.BlockSpec((1, n_heads, seq), lambda i, j: (i, 0, 0)),
                  pl.BlockSpec((1, tq, LANES), lambda i, j: (i, j, 0))],
        out_specs=pl.BlockSpec((1, tq, width), lambda i, j: (i, j, 0)),
        out_shape=jax.ShapeDtypeStruct((b, seq, width), jnp.bfloat16),
        scratch_shapes=[pltpu.VMEM((tq, 1), jnp.float32),
                        pltpu.VMEM((tq, 1), jnp.float32),
                        pltpu.VMEM((tq, HEAD_DIM), jnp.float32)],
        compiler_params=_cparams(2),
        name=name,
    )(q, k, v, ft, fnat)


def _lambda(lq1_ref, lk1_ref, lq2_ref, lk2_ref, lam_init):
    d1 = jnp.sum(lq1_ref[...] * lk1_ref[...], axis=-1, keepdims=True)
    d2 = jnp.sum(lq2_ref[...] * lk2_ref[...], axis=-1, keepdims=True)
    return jnp.exp(d1) - jnp.exp(d2) + lam_init


def _subln(o, g, lam_init):
    ms = jnp.mean(o * o, axis=-1, keepdims=True)
    return o * lax.rsqrt(ms + SUBLN_EPS) * g * (1.0 - lam_init)


def _diff_attn_kernel(q_ref, k_ref, v_ref, lq1_ref, lk1_ref, lq2_ref, lk2_ref, g_ref, o_ref,
                      m_ref, l_ref, acc_ref, *, tq, n_heads, lam_init):
    qi = pl.program_id(1)
    mask = _causal_mask(tq, tq)
    lam = _lambda(lq1_ref, lk1_ref, lq2_ref, lk2_ref, lam_init)
    dv = 2 * HEAD_DIM
    for h in range(n_heads):
        vsl = slice(h * dv, (h + 1) * dv)
        for c in range(2):
            sl = slice((2 * h + c) * HEAD_DIM, (2 * h + c + 1) * HEAD_DIM)
            q = q_ref[0, :, sl]
            m_c, l_c, acc_c = m_ref.at[c], l_ref.at[c], acc_ref.at[c]
            m_c[...] = jnp.full(m_c.shape, -jnp.inf, jnp.float32)
            l_c[...] = jnp.zeros(l_c.shape, jnp.float32)
            acc_c[...] = jnp.zeros(acc_c.shape, jnp.float32)

            def logits(start, q=q, sl=sl, vsl=vsl):
                k = k_ref[0, pl.ds(start, tq), sl]
                s = lax.dot_general(q, k, _NT, preferred_element_type=jnp.float32)
                return s, v_ref[0, pl.ds(start, tq), vsl]

            def body(kj, carry, logits=logits, m_c=m_c, l_c=l_c, acc_c=acc_c):
                t, v = logits(pl.multiple_of(kj * tq, tq))
                _softmax_step(t, v, m_c, l_c, acc_c)
                return carry

            lax.fori_loop(0, qi, body, 0)
            t, v = logits(pl.multiple_of(qi * tq, tq))
            _softmax_step(jnp.where(mask, t, -jnp.inf), v, m_c, l_c, acc_c)
        o = acc_ref[0] / l_ref[0] - lam * (acc_ref[1] / l_ref[1])
        o_ref[0, :, vsl] = _subln(o, g_ref[...], lam_init).astype(o_ref.dtype)


def diff_attention(q, k, v, lam_vecs, g_subln, lam_init, n_heads, name, tq=512):
    b, seq, width = q.shape
    vec = pl.BlockSpec((1, HEAD_DIM), lambda i, j: (0, 0))
    return pl.pallas_call(
        functools.partial(_diff_attn_kernel, tq=tq, n_heads=n_heads, lam_init=lam_init),
        grid=(b, seq // tq),
        in_specs=[pl.BlockSpec((1, tq, width), lambda i, j: (i, j, 0)),
                  pl.BlockSpec((1, seq, width), lambda i, j: (i, 0, 0)),
                  pl.BlockSpec((1, seq, width), lambda i, j: (i, 0, 0)),
                  vec, vec, vec, vec,
                  pl.BlockSpec((1, 2 * HEAD_DIM), lambda i, j: (0, 0))],
        out_specs=pl.BlockSpec((1, tq, width), lambda i, j: (i, j, 0)),
        out_shape=jax.ShapeDtypeStruct((b, seq, width), jnp.bfloat16),
        scratch_shapes=[pltpu.VMEM((2, tq, 1), jnp.float32),
                        pltpu.VMEM((2, tq, 1), jnp.float32),
                        pltpu.VMEM((2, tq, 2 * HEAD_DIM), jnp.float32)],
        compiler_params=_cparams(2),
        name=name,
    )(q, k, v, *lam_vecs, g_subln)


def _decode_kernel(pt_ref, qf_ref, qd_ref, kfn_ref, vfn_ref, kdn_ref, vdn_ref, lfn_ref,
                   lq1_ref, lk1_ref, lq2_ref, lk2_ref, g_ref, *rest,
                   n_tok, pages_per_step, lam_init):
    n_cache = 5 * pages_per_step
    cache = rest[:n_cache]
    fo_ref, do_ref = rest[n_cache:n_cache + 2]
    (new_ref, crow_ref, off_ref, mf_ref, lf_ref, accf_ref, md_ref, ld_ref, accd_ref) = rest[n_cache + 2:]
    step = pl.program_id(1)
    page = LANES
    n_diff = GROUPS // 2
    bf = jnp.bfloat16
    row = lax.broadcasted_iota(jnp.int32, (Q_ROWS, page), 0)
    lane = lax.broadcasted_iota(jnp.int32, (Q_ROWS, page), 1)
    causal = lane <= row

    def attend(q, k, v, bias, is_new, m_r, l_r, acc_r):
        s = lax.dot_general(q, k, _NT, preferred_element_type=jnp.float32)
        if bias is not None:
            s = s + bias
        if is_new:
            s = jnp.where(causal, s, -jnp.inf)
        _softmax_step(s, v, m_r, l_r, acc_r)

    def visit_page(slab, key_term, is_new):
        for h in range(GROUPS):
            bias = crow_ref[h] + key_term[h:h + 1, :]
            attend(qf_ref[0, h], slab(0, h).astype(bf), slab(1, h).astype(bf), bias, is_new,
                   mf_ref.at[h], lf_ref.at[h], accf_ref.at[h])
        for h in range(n_diff):
            v = jnp.concatenate([slab(3, h).astype(bf), slab(3, n_diff + h).astype(bf)], axis=1)
            for c in range(2):
                g = 2 * h + c
                attend(qd_ref[0, g], slab(2, g).astype(bf), v, None, is_new,
                       md_ref.at[g], ld_ref.at[g], accd_ref.at[g])

    @pl.when(step == 0)
    def _first():
        for r in (mf_ref, md_ref):
            r[...] = jnp.full(r.shape, -jnp.inf, jnp.float32)
        for r in (lf_ref, ld_ref, accf_ref, accd_ref, off_ref):
            r[...] = jnp.zeros(r.shape, jnp.float32)
        new_ref[...] = jnp.zeros(new_ref.shape, jnp.float32)
        for i, src in enumerate((kfn_ref, vfn_ref, kdn_ref, vdn_ref)):
            new_ref[i, 0:n_tok * GROUPS, :] = src[0]
        pref = _lane_prefix(lfn_ref[0])
        for h in range(GROUPS):
            picked = jnp.where(lane == row, pref[h:h + 1, :], 0.0)
            crow_ref[h] = jnp.sum(picked, axis=-1, keepdims=True)
        visit_page(lambda i, g: new_ref[i, pl.ds(g, page, stride=GROUPS), :], -pref, True)

    for slot in range(pages_per_step):
        refs = cache[5 * slot:5 * slot + 5]
        kv_refs = (refs[0], refs[1], refs[3], refs[4])
        pref = _lane_prefix(refs[2][0, 0])
        total = pref[:, page - 1:page]
        key_term = total - pref + off_ref[...]
        off_ref[...] = off_ref[...] + total
        visit_page(lambda i, g, kv_refs=kv_refs: kv_refs[i][0, 0, pl.ds(g, page, stride=GROUPS), :],
                   key_term, False)

    @pl.when(step == pl.num_programs(1) - 1)
    def _last():
        for h in range(GROUPS):
            o = accf_ref[h] / lf_ref[h]
            fo_ref[0, :, h * HEAD_DIM:(h + 1) * HEAD_DIM] = o[0:n_tok]
        lam = _lambda(lq1_ref, lk1_ref, lq2_ref, lk2_ref, lam_init)
        dv = 2 * HEAD_DIM
        for h in range(n_diff):
            o = accd_ref[2 * h] / ld_ref[2 * h] - lam * (accd_ref[2 * h + 1] / ld_ref[2 * h + 1])
            do_ref[0, :, h * dv:(h + 1) * dv] = _subln(o, g_ref[...], lam_init)[0:n_tok]


def decode_attention(layer, page_table, q_fox, q_diff, new_kv, new_logf_t, caches, lam_vecs, g_subln,
                     lam_init, name):
    b, n_pages = page_table.shape
    n_tok = new_kv[0].shape[1] // GROUPS
    width = GROUPS * HEAD_DIM
    page = caches[0].shape[2] // GROUPS
    assert page == LANES
    pps = PAGES_PER_STEP
    assert n_pages % pps == 0

    def per_batch(shape):
        zeros = (0,) * len(shape)
        return pl.BlockSpec((1,) + shape, lambda i, s, pt: (i,) + zeros)

    def const(shape):
        return pl.BlockSpec(shape, lambda i, s, pt: (0, 0))

    def cache_spec(arr, slot):
        def imap(i, s, pt):
            return (layer, pt[i, n_pages - 1 - (s * pps + slot)], 0, 0)
        return pl.BlockSpec((1, 1) + arr.shape[2:], imap)

    in_specs = [per_batch((GROUPS, Q_ROWS, HEAD_DIM))] * 2
    in_specs += [per_batch((n_tok * GROUPS, LANES))] * 4
    in_specs += [per_batch((GROUPS, LANES))]
    in_specs += [const((1, HEAD_DIM))] * 4 + [const((1, 2 * HEAD_DIM))]
    args = [q_fox, q_diff, *new_kv, new_logf_t, *lam_vecs, g_subln]
    for slot in range(pps):
        for arr in caches:
            in_specs.append(cache_spec(arr, slot))
            args.append(arr)
    out_spec = pl.BlockSpec((1, n_tok, width), lambda i, s, pt: (i, 0, 0))
    stat = pltpu.VMEM((GROUPS, Q_ROWS, 1), jnp.float32)
    grid_spec = pltpu.PrefetchScalarGridSpec(
        num_scalar_prefetch=1,
        grid=(b, n_pages // pps),
        in_specs=in_specs,
        out_specs=[out_spec, out_spec],
        scratch_shapes=[pltpu.VMEM((4, page * GROUPS, LANES), jnp.float32),
                        stat,
                        pltpu.VMEM((GROUPS, 1), jnp.float32),
                        stat, stat, pltpu.VMEM((GROUPS, Q_ROWS, HEAD_DIM), jnp.float32),
                        stat, stat, pltpu.VMEM((GROUPS, Q_ROWS, 2 * HEAD_DIM), jnp.float32)])
    out = jax.ShapeDtypeStruct((b, n_tok, width), jnp.float32)
    return pl.pallas_call(
        functools.partial(_decode_kernel, n_tok=n_tok, pages_per_step=pps, lam_init=lam_init),
        grid_spec=grid_spec,
        out_shape=[out, out],
        compiler_params=_cparams(2),
        name=name,
    )(page_table, *args)


def _rope_tables(pos):
    half = ROT_DIM // 2
    inv = ROPE_THETA ** (-jnp.arange(half, dtype=jnp.float32) / half)
    ang = pos.astype(jnp.float32)[:, None] * inv[None, :]
    cos, sin = jnp.cos(ang), jnp.sin(ang)
    n = pos.shape[0]
    one = jnp.ones((n, LANES - ROT_DIM), jnp.float32)
    zero_half = jnp.zeros((n, half), jnp.float32)
    zero_rest = jnp.zeros((n, LANES - ROT_DIM), jnp.float32)
    return (jnp.concatenate([cos, cos, one], axis=1),
            jnp.concatenate([-sin, zero_half, zero_rest], axis=1),
            jnp.concatenate([zero_half, sin, zero_rest], axis=1))


def _project(h, w, rope, tag, bm):
    bf, f32 = jnp.bfloat16, jnp.float32
    mm = functools.partial(matmul, h, bm=bm, bn=1024, w_transposed=True)
    fq = mm(w["q"], epilogue="plain", out_dtypes=[bf], scale=ATTN_SCALE, name=f"{tag}_fq")
    dq = mm(w["dq"], epilogue="plain", out_dtypes=[bf], scale=ATTN_SCALE, rope_tables=rope, name=f"{tag}_dq")
    heads = functools.partial(mm, epilogue="heads", out_dtypes=[f32, bf])
    fk = heads(w["k"], head_rows=_ROWS_IDENTITY, name=f"{tag}_fk")
    fv = heads(w["v"], head_rows=_ROWS_IDENTITY, name=f"{tag}_fv")
    dk = heads(w["dk"], head_rows=_ROWS_IDENTITY, rope_tables=rope, name=f"{tag}_dk")
    dv = heads(w["dv"], head_rows=_ROWS_DIFF_V, name=f"{tag}_dv")
    gates = mm(w["g"], epilogue="sigmoid", out_dtypes=[f32], name=f"{tag}_gates")
    logf = mm(w["f"], epilogue="log_sigmoid", out_dtypes=[f32],
              extras=[(w["bf"], (1, LANES), lambda i, j: (0, 0))], name=f"{tag}_logf")
    return fq, dq, fk, fv, dk, dv, gates, logf


def _finish(x, fo, do, gates, w, tag):
    m, d = x.shape
    mix = gated_mix(fo, do, w["fox_out"], w["diff_out"], gates, name=f"{tag}_mix")
    bm = min(512, m)
    x2, h2 = matmul(mix, w["o"], epilogue="residual_norm", out_dtypes=[jnp.float32, jnp.bfloat16],
                    bm=bm, bn=d, name=f"{tag}_wo",
                    extras=[(x, (bm, d), lambda i, j: (i, 0)),
                            (w["g_mlp"], (1, d), lambda i, j: (0, 0))])
    u = matmul(h2, w["up"], epilogue="relu2", out_dtypes=[jnp.bfloat16], bm=1024, bn=1024, name=f"{tag}_up")
    return matmul(u, w["down"], epilogue="residual", out_dtypes=[jnp.float32], bm=bm, bn=512,
                  name=f"{tag}_down", extras=[(x2, (bm, 512), lambda i, j: (i, j))])


def kernel(x_prompt, x_sample, cache_fox_k, cache_fox_v, cache_fox_logf, cache_diff_k, cache_diff_v,
           page_table, g_attn, w_in, b_f, lam_q1, lam_k1, lam_q2, lam_k2, g_subln,
           w_fox_out, w_diff_out, w_o, g_mlp, w_up, w_down, g_final):
    bp, seq, d = x_prompt.shape
    bs, n_tok, _ = x_sample.shape
    depth, n_pool, page, n_fox, _ = cache_fox_k.shape
    n_diff = cache_diff_k.shape[3]
    assert n_fox == GROUPS and 2 * n_diff == GROUPS
    fox_w = n_fox * HEAD_DIM
    diff_w = n_diff * 2 * HEAD_DIM
    n_pages = page_table.shape[1]
    past_len = n_pages * page
    bf = jnp.bfloat16

    flat = (depth, n_pool, page * GROUPS, HEAD_DIM)
    dv_cache = cache_diff_v.reshape(depth, n_pool, page, n_diff, 2, HEAD_DIM)
    caches = (cache_fox_k.reshape(flat), cache_fox_v.reshape(flat),
              jnp.swapaxes(cache_fox_logf, 2, 3),
              cache_diff_k.reshape(flat),
              jnp.swapaxes(dv_cache, 3, 4).reshape(flat))

    bm_p = 1024
    rope_p = (_rope_tables(jnp.arange(seq)), seq // bm_p)
    rope_s = (_rope_tables(past_len + jnp.arange(bs * n_tok) % n_tok), 1)

    xp = x_prompt.reshape(bp * seq, d)
    xs = x_sample.reshape(bs * n_tok, d)
    w_in_t = jnp.swapaxes(w_in, 1, 2)
    outs_p = [[] for _ in range(5)]
    outs_s = [[] for _ in range(5)]

    for l in range(depth):
        lam_init = 0.8 - 0.6 * math.exp(-0.3 * l)
        o = 0
        parts = {}
        for key, size in (("q", fox_w), ("k", fox_w), ("v", fox_w), ("f", n_fox),
                          ("dq", diff_w), ("dk", diff_w), ("dv", diff_w), ("g", 2 * d)):
            parts[key] = w_in_t[l, o:o + size].astype(bf)
            o += size
        parts["f"] = jnp.pad(parts["f"], ((0, LANES - n_fox), (0, 0)))
        parts["bf"] = jnp.pad(b_f[l], (0, LANES - n_fox)).reshape(1, LANES)
        parts.update(fox_out=w_fox_out[l].astype(bf), diff_out=w_diff_out[l].astype(bf),
                     o=w_o[l].astype(bf), up=w_up[l].astype(bf), down=w_down[l].astype(bf),
                     g_mlp=g_mlp[l].reshape(1, d))
        lam_vecs = [v[l].reshape(1, HEAD_DIM) for v in (lam_q1, lam_k1, lam_q2, lam_k2)]
        gs = g_subln[l].reshape(1, 2 * HEAD_DIM)

        hp = rmsnorm(xp, g_attn[l], NORM_EPS, bf, name=f"p{l}_norm")
        fq, dq, (fk, fkb), (fv, fvb), (dk, dkb), (dv, dvb), gates, logf = _project(hp, parts, rope_p, f"p{l}", bm_p)
        ft, fnat = fox_prep(logf.reshape(bp, seq, LANES), n_fox, name=f"p{l}_foxprep")
        shp = lambda a: a.reshape(bp, seq, -1)
        fo = fox_attention(shp(fq), shp(fkb), shp(fvb), ft, fnat, n_fox, name=f"p{l}_fox")
        do = diff_attention(shp(dq), shp(dkb), shp(dvb), lam_vecs, gs, lam_init, n_diff, name=f"p{l}_diff")
        xp = _finish(xp, fo.reshape(bp * seq, fox_w), do.reshape(bp * seq, diff_w), gates, parts, f"p{l}")
        for lst, a in zip(outs_p, (fk, fv, logf[:, :n_fox], dk, dv)):
            lst.append(a)

        hs = rmsnorm(xs, g_attn[l], NORM_EPS, bf, name=f"s{l}_norm")
        fq, dq, (fk, _), (fv, _), (dk, _), (dv, _), gates, logf = _project(hs, parts, rope_s, f"s{l}", 512)

        def group_queries(q):
            q = jnp.swapaxes(q.reshape(bs, n_tok, GROUPS, HEAD_DIM), 1, 2)
            return jnp.pad(q, ((0, 0), (0, 0), (0, Q_ROWS - n_tok), (0, 0)))

        new_kv = [a.reshape(bs, n_tok * GROUPS, HEAD_DIM) for a in (fk, fv, dk, dv)]
        lf_t = jnp.swapaxes(logf[:, :n_fox].reshape(bs, n_tok, n_fox), 1, 2)
        lf_t = jnp.pad(lf_t, ((0, 0), (0, 0), (0, LANES - n_tok)))
        fo, do = decode_attention(l, page_table, group_queries(fq), group_queries(dq), new_kv, lf_t,
                                  caches, lam_vecs, gs, lam_init, name=f"s{l}_decode")
        xs = _finish(xs, fo.reshape(bs * n_tok, fox_w).astype(bf), do.reshape(bs * n_tok, diff_w).astype(bf),
                     gates, parts, f"s{l}")
        for lst, a in zip(outs_s, (fk, fv, logf[:, :n_fox], dk, dv)):
            lst.append(a)

    y_p = rmsnorm(xp, g_final, NORM_EPS, jnp.float32, name="p_final_norm").reshape(bp, seq, d)
    y_s = rmsnorm(xs, g_final, NORM_EPS, jnp.float32, name="s_final_norm").reshape(bs, n_tok, d)

    def assemble(lists, lead):
        fk, fv, lf, dk, dv = (jnp.stack(lst) for lst in lists)
        dv = jnp.swapaxes(dv.reshape((depth,) + lead + (2, n_diff, HEAD_DIM)), -3, -2)
        return (fk.reshape((depth,) + lead + (n_fox, HEAD_DIM)),
                fv.reshape((depth,) + lead + (n_fox, HEAD_DIM)),
                lf.reshape((depth,) + lead + (n_fox,)),
                dk.reshape((depth,) + lead + (n_diff, 2, HEAD_DIM)),
                dv.reshape((depth,) + lead + (n_diff, 2 * HEAD_DIM)))

    return (y_p, y_s, *assemble(outs_p, (bp, seq)), *assemble(outs_s, (bs, n_tok)))
```

```python
import functools
import math

import jax
import jax.numpy as jnp
from jax import lax
from jax.experimental import pallas as pl
from jax.experimental.pallas import tpu as pltpu

HEAD_DIM = 128
ROT_DIM = HEAD_DIM // 4
ROPE_THETA = 500000.0
ATTN_SCALE = HEAD_DIM ** -0.5
NORM_EPS = 1e-6
SUBLN_EPS = 1e-5
LANES = 128
GROUPS = 8
VMEM_LIMIT_BYTES = 56 * 1024 * 1024
PAGES_PER_STEP = 8
Q_ROWS = 16

_NT = (((1,), (1,)), ((), ()))

_ROWS_IDENTITY = tuple(range(GROUPS))
_ROWS_DIFF_V = tuple((j % 2) * 4 + j // 2 for j in range(GROUPS))


def _cparams(n_axes):
    return pltpu.CompilerParams(dimension_semantics=("arbitrary",) * n_axes,
                                vmem_limit_bytes=VMEM_LIMIT_BYTES)


def _rmsnorm_kernel(x_ref, g_ref, o_ref, *, eps):
    x = x_ref[...]
    ms = jnp.mean(x * x, axis=-1, keepdims=True)
    o_ref[...] = (x * lax.rsqrt(ms + eps) * g_ref[...]).astype(o_ref.dtype)


def rmsnorm(x, g, eps, out_dtype, name):
    m, d = x.shape
    bm = min(512, m)
    return pl.pallas_call(
        functools.partial(_rmsnorm_kernel, eps=eps),
        grid=(m // bm,),
        in_specs=[pl.BlockSpec((bm, d), lambda i: (i, 0)),
                  pl.BlockSpec((1, d), lambda i: (0, 0))],
        out_specs=pl.BlockSpec((bm, d), lambda i: (i, 0)),
        out_shape=jax.ShapeDtypeStruct((m, d), out_dtype),
        compiler_params=_cparams(1),
        name=name,
    )(x, g.reshape(1, d))


def _rope(acc, cos, sin_lo, sin_hi):
    half = ROT_DIM // 2
    chunks = []
    for c in range(acc.shape[1] // LANES):
        x = acc[:, c * LANES:(c + 1) * LANES]
        up = pltpu.roll(x, LANES - half, 1)
        dn = pltpu.roll(x, half, 1)
        chunks.append(x * cos + up * sin_lo + dn * sin_hi)
    return jnp.concatenate(chunks, axis=1)


def _mm_kernel(x_ref, w_ref, *refs, epilogue, scale, rope, w_transposed, head_rows):
    if w_transposed:
        acc = lax.dot_general(x_ref[...], w_ref[...], _NT, preferred_element_type=jnp.float32)
    else:
        acc = jnp.dot(x_ref[...], w_ref[...], preferred_element_type=jnp.float32)
    if scale != 1.0:
        acc = acc * scale
    if rope:
        acc = _rope(acc, refs[0][...], refs[1][...], refs[2][...])
        refs = refs[3:]
    if epilogue == "plain":
        for o_ref in refs:
            o_ref[...] = acc.astype(o_ref.dtype)
    elif epilogue == "heads":
        hm_ref, o_ref = refs
        bm = acc.shape[0]
        for j, r in enumerate(head_rows):
            hm_ref[pl.ds(r, bm, stride=GROUPS), :] = acc[:, j * LANES:(j + 1) * LANES]
        o_ref[...] = acc.astype(o_ref.dtype)
    elif epilogue == "sigmoid":
        (o_ref,) = refs
        o_ref[...] = jax.nn.sigmoid(acc).astype(o_ref.dtype)
    elif epilogue == "log_sigmoid":
        b_ref, o_ref = refs
        o_ref[...] = jax.nn.log_sigmoid(acc + b_ref[...]).astype(o_ref.dtype)
    elif epilogue == "relu2":
        (o_ref,) = refs
        r = jnp.maximum(acc, 0.0)
        o_ref[...] = (r * r).astype(o_ref.dtype)
    elif epilogue == "residual":
        res_ref, o_ref = refs
        o_ref[...] = res_ref[...] + acc
    elif epilogue == "residual_norm":
        res_ref, g_ref, x_ref_out, h_ref_out = refs
        xn = res_ref[...] + acc
        x_ref_out[...] = xn
        ms = jnp.mean(xn * xn, axis=-1, keepdims=True)
        h_ref_out[...] = (xn * lax.rsqrt(ms + NORM_EPS) * g_ref[...]).astype(h_ref_out.dtype)
    else:
        raise ValueError(epilogue)


def matmul(x, w, *, epilogue, out_dtypes, name, bm, bn, scale=1.0, extras=(), rope_tables=None,
           w_transposed=False, head_rows=None):
    m, k = x.shape
    n = w.shape[0] if w_transposed else w.shape[1]
    bm, bn = min(bm, m), min(bn, n)
    assert m % bm == 0 and n % bn == 0
    in_specs = [pl.BlockSpec((bm, k), lambda i, j: (i, 0)),
                pl.BlockSpec((bn, k), lambda i, j: (j, 0)) if w_transposed
                else pl.BlockSpec((k, bn), lambda i, j: (0, j))]
    args = [x, w]
    if rope_tables is not None:
        tables, n_blocks = rope_tables
        for t in tables:
            in_specs.append(pl.BlockSpec((bm, LANES), lambda i, j: (i % n_blocks, 0)))
            args.append(t)
    for arr, blk, imap in extras:
        in_specs.append(pl.BlockSpec(blk, imap))
        args.append(arr)
    if epilogue == "heads":
        assert bn == GROUPS * LANES == n
        out_specs = [pl.BlockSpec((bm * GROUPS, LANES), lambda i, j: (i, 0)),
                     pl.BlockSpec((bm, bn), lambda i, j: (i, j))]
        out_shape = [jax.ShapeDtypeStruct((m * GROUPS, LANES), out_dtypes[0]),
                     jax.ShapeDtypeStruct((m, n), out_dtypes[1])]
    else:
        out_specs = [pl.BlockSpec((bm, bn), lambda i, j: (i, j)) for _ in out_dtypes]
        out_shape = [jax.ShapeDtypeStruct((m, n), dt) for dt in out_dtypes]
    outs = pl.pallas_call(
        functools.partial(_mm_kernel, epilogue=epilogue, scale=scale, rope=rope_tables is not None,
                          w_transposed=w_transposed, head_rows=head_rows),
        grid=(m // bm, n // bn),
        in_specs=in_specs, out_specs=out_specs, out_shape=out_shape,
        compiler_params=_cparams(2),
        name=name,
    )(*args)
    return outs[0] if len(outs) == 1 else outs


def _mix_kernel(fo_ref, do_ref, wf_ref, wd_ref, g0_ref, g1_ref, o_ref):
    a = jnp.dot(fo_ref[...], wf_ref[...], preferred_element_type=jnp.float32)
    b = jnp.dot(do_ref[...], wd_ref[...], preferred_element_type=jnp.float32)
    o_ref[...] = (g0_ref[...] * a + g1_ref[...] * b).astype(o_ref.dtype)


def gated_mix(fo, do, w_fox_out, w_diff_out, gates, name):
    m, kf = fo.shape
    kd = do.shape[1]
    d = w_fox_out.shape[1]
    bm, bn = min(1024, m), 512
    nj = d // bn
    return pl.pallas_call(
        _mix_kernel,
        grid=(m // bm, nj),
        in_specs=[pl.BlockSpec((bm, kf), lambda i, j: (i, 0)),
                  pl.BlockSpec((bm, kd), lambda i, j: (i, 0)),
                  pl.BlockSpec((kf, bn), lambda i, j: (0, j)),
                  pl.BlockSpec((kd, bn), lambda i, j: (0, j)),
                  pl.BlockSpec((bm, bn), lambda i, j: (i, j)),
                  pl.BlockSpec((bm, bn), lambda i, j: (i, j + nj))],
        out_specs=pl.BlockSpec((bm, bn), lambda i, j: (i, j)),
        out_shape=jax.ShapeDtypeStruct((m, d), jnp.bfloat16),
        compiler_params=_cparams(2),
        name=name,
    )(fo, do, w_fox_out, w_diff_out, gates, gates)


def _lane_prefix(x):
    lane = lax.broadcasted_iota(jnp.int32, x.shape, 1)
    shift = 1
    while shift < LANES:
        x = x + jnp.where(lane >= shift, pltpu.roll(x, shift, 1), 0.0)
        shift *= 2
    return x


def _fox_prep_kernel(lf_ref, ft_ref, fn_ref, *, n_heads):
    seq = lf_ref.shape[1]
    lt = lf_ref[0].T[:n_heads, :]
    blocks = [_lane_prefix(lt[:, c * LANES:(c + 1) * LANES]) for c in range(seq // LANES)]
    carry = jnp.zeros((n_heads, 1), jnp.float32)
    outs = []
    for blk in blocks:
        blk = blk + carry
        carry = blk[:, LANES - 1:LANES]
        outs.append(blk)
    ft = jnp.concatenate(outs, axis=1)
    ft_ref[0] = ft
    padded = jnp.concatenate([ft, jnp.zeros((LANES - n_heads, seq), jnp.float32)], axis=0)
    fn_ref[0] = padded.T


def fox_prep(logf_pad, n_heads, name):
    b, seq, _ = logf_pad.shape
    return pl.pallas_call(
        functools.partial(_fox_prep_kernel, n_heads=n_heads),
        grid=(b,),
        in_specs=[pl.BlockSpec((1, seq, LANES), lambda i: (i, 0, 0))],
        out_specs=[pl.BlockSpec((1, n_heads, seq), lambda i: (i, 0, 0)),
                   pl.BlockSpec((1, seq, LANES), lambda i: (i, 0, 0))],
        out_shape=[jax.ShapeDtypeStruct((b, n_heads, seq), jnp.float32),
                   jax.ShapeDtypeStruct((b, seq, LANES), jnp.float32)],
        compiler_params=_cparams(1),
        name=name,
    )(logf_pad)


def _softmax_step(t, v, m_ref, l_ref, acc_ref, row_shift=None):
    t_max = jnp.max(t, axis=-1, keepdims=True)
    if row_shift is not None:
        t_max = t_max + row_shift
    m_prev = m_ref[...]
    m_new = jnp.maximum(m_prev, t_max)
    alpha = jnp.exp(m_prev - m_new)
    p = jnp.exp(t - (m_new - row_shift if row_shift is not None else m_new))
    l_ref[...] = alpha * l_ref[...] + jnp.sum(p, axis=-1, keepdims=True)
    acc_ref[...] = alpha * acc_ref[...] + jnp.dot(p.astype(v.dtype), v,
                                                  preferred_element_type=jnp.float32)
    m_ref[...] = m_new


def _causal_blocks(qi, tq, tk):
    assert tk % tq == 0
    n_full = qi // (tk // tq)
    lead = qi * tq - n_full * tk
    col_minus_row = (lax.broadcasted_iota(jnp.int32, (tq, tk), 1)
                     - lax.broadcasted_iota(jnp.int32, (tq, tk), 0))
    return n_full, col_minus_row <= lead


def _fox_attn_kernel(q_ref, k_ref, v_ref, ft_ref, fn_ref, o_ref, m_ref, l_ref, acc_ref, *, tq, tk, n_heads):
    n_full, mask = _causal_blocks(pl.program_id(1), tq, tk)
    for h in range(n_heads):
        sl = slice(h * HEAD_DIM, (h + 1) * HEAD_DIM)
        q = q_ref[0, :, sl]
        fq = fn_ref[0, :, h:h + 1]
        m_ref[...] = jnp.full(m_ref.shape, -jnp.inf, jnp.float32)
        l_ref[...] = jnp.zeros(l_ref.shape, jnp.float32)
        acc_ref[...] = jnp.zeros(acc_ref.shape, jnp.float32)

        def logits(start, q=q, sl=sl, h=h):
            k = k_ref[0, pl.ds(start, tk), sl]
            fk = ft_ref[0, h:h + 1, pl.ds(start, tk)]
            s = lax.dot_general(q, k, _NT, preferred_element_type=jnp.float32)
            return s - fk, v_ref[0, pl.ds(start, tk), sl]

        def body(kj, carry, logits=logits, fq=fq):
            t, v = logits(pl.multiple_of(kj * tk, tk))
            _softmax_step(t, v, m_ref, l_ref, acc_ref, row_shift=fq)
            return carry

        lax.fori_loop(0, n_full, body, 0)
        t, v = logits(pl.multiple_of(n_full * tk, tk))
        _softmax_step(jnp.where(mask, t, -jnp.inf), v, m_ref, l_ref, acc_ref, row_shift=fq)
        o_ref[0, :, sl] = (acc_ref[...] / l_ref[...]).astype(o_ref.dtype)


def fox_attention(q, k, v, ft, fnat, n_heads, name, tq=512, tk=1024):
    b, seq, width = q.shape
    return pl.pallas_call(
        functools.partial(_fox_attn_kernel, tq=tq, tk=tk, n_heads=n_heads),
        grid=(b, seq // tq),
        in_specs=[pl.BlockSpec((1, tq, width), lambda i, j: (i, j, 0)),
                  pl.BlockSpec((1, seq, width), lambda i, j: (i, 0, 0)),
                  pl.BlockSpec((1, seq, width), lambda i, j: (i, 0, 0)),
                  pl.BlockSpec((1, n_heads, seq), lambda i, j: (i, 0, 0)),
                  pl.BlockSpec((1, tq, LANES), lambda i, j: (i, j, 0))],
        out_specs=pl.BlockSpec((1, tq, width), lambda i, j: (i, j, 0)),
        out_shape=jax.ShapeDtypeStruct((b, seq, width), jnp.bfloat16),
        scratch_shapes=[pltpu.VMEM((tq, 1), jnp.float32),
                        pltpu.VMEM((tq, 1), jnp.float32),
                        pltpu.VMEM((tq, HEAD_DIM), jnp.float32)],
        compiler_params=_cparams(2),
        name=name,
    )(q, k, v, ft, fnat)


def _lambda(lq1_ref, lk1_ref, lq2_ref, lk2_ref, lam_init):
    d1 = jnp.sum(lq1_ref[...] * lk1_ref[...], axis=-1, keepdims=True)
    d2 = jnp.sum(lq2_ref[...] * lk2_ref[...], axis=-1, keepdims=True)
    return jnp.exp(d1) - jnp.exp(d2) + lam_init


def _subln(o, g, lam_init):
    ms = jnp.mean(o * o, axis=-1, keepdims=True)
    return o * lax.rsqrt(ms + SUBLN_EPS) * g * (1.0 - lam_init)


def _diff_attn_kernel(q_ref, k_ref, v_ref, lq1_ref, lk1_ref, lq2_ref, lk2_ref, g_ref, o_ref,
                      m_ref, l_ref, acc_ref, *, tq, tk, n_heads, lam_init):
    n_full, mask = _causal_blocks(pl.program_id(1), tq, tk)
    lam = _lambda(lq1_ref, lk1_ref, lq2_ref, lk2_ref, lam_init)
    dv = 2 * HEAD_DIM
    for h in range(n_heads):
        vsl = slice(h * dv, (h + 1) * dv)
        for c in range(2):
            sl = slice((2 * h + c) * HEAD_DIM, (2 * h + c + 1) * HEAD_DIM)
            q = q_ref[0, :, sl]
            m_c, l_c, acc_c = m_ref.at[c], l_ref.at[c], acc_ref.at[c]
            m_c[...] = jnp.full(m_c.shape, -jnp.inf, jnp.float32)
            l_c[...] = jnp.zeros(l_c.shape, jnp.float32)
            acc_c[...] = jnp.zeros(acc_c.shape, jnp.float32)

            def logits(start, q=q, sl=sl, vsl=vsl):
                k = k_ref[0, pl.ds(start, tk), sl]
                s = lax.dot_general(q, k, _NT, preferred_element_type=jnp.float32)
                return s, v_ref[0, pl.ds(start, tk), vsl]

            def body(kj, carry, logits=logits, m_c=m_c, l_c=l_c, acc_c=acc_c):
                t, v = logits(pl.multiple_of(kj * tk, tk))
                _softmax_step(t, v, m_c, l_c, acc_c)
                return carry

            lax.fori_loop(0, n_full, body, 0)
            t, v = logits(pl.multiple_of(n_full * tk, tk))
            _softmax_step(jnp.where(mask, t, -jnp.inf), v, m_c, l_c, acc_c)
        o = acc_ref[0] / l_ref[0] - lam * (acc_ref[1] / l_ref[1])
        o_ref[0, :, vsl] = _subln(o, g_ref[...], lam_init).astype(o_ref.dtype)


def diff_attention(q, k, v, lam_vecs, g_subln, lam_init, n_heads, name, tq=512, tk=1024):
    b, seq, width = q.shape
    vec = pl.BlockSpec((1, HEAD_DIM), lambda i, j: (0, 0))
    return pl.pallas_call(
        functools.partial(_diff_attn_kernel, tq=tq, tk=tk, n_heads=n_heads, lam_init=lam_init),
        grid=(b, seq // tq),
        in_specs=[pl.BlockSpec((1, tq, width), lambda i, j: (i, j, 0)),
                  pl.BlockSpec((1, seq, width), lambda i, j: (i, 0, 0)),
                  pl.BlockSpec((1, seq, width), lambda i, j: (i, 0, 0)),
                  vec, vec, vec, vec,
                  pl.BlockSpec((1, 2 * HEAD_DIM), lambda i, j: (0, 0))],
        out_specs=pl.BlockSpec((1, tq, width), lambda i, j: (i, j, 0)),
        out_shape=jax.ShapeDtypeStruct((b, seq, width), jnp.bfloat16),
        scratch_shapes=[pltpu.VMEM((2, tq, 1), jnp.float32),
                        pltpu.VMEM((2, tq, 1), jnp.float32),
                        pltpu.VMEM((2, tq, 2 * HEAD_DIM), jnp.float32)],
        compiler_params=_cparams(2),
        name=name,
    )(q, k, v, *lam_vecs, g_subln)


def _decode_kernel(pt_ref, qf_ref, qd_ref, kfn_ref, vfn_ref, kdn_ref, vdn_ref, lfn_ref,
                   lq1_ref, lk1_ref, lq2_ref, lk2_ref, g_ref, *rest,
                   n_tok, pages_per_step, lam_init):
    n_cache = 5 * pages_per_step
    cache = rest[:n_cache]
    fo_ref, do_ref = rest[n_cache:n_cache + 2]
    (new_ref, crow_ref, off_ref, mf_ref, lf_ref, accf_ref, md_ref, ld_ref, accd_ref) = rest[n_cache + 2:]
    step = pl.program_id(1)
    page = LANES
    n_diff = GROUPS // 2
    bf = jnp.bfloat16
    row = lax.broadcasted_iota(jnp.int32, (Q_ROWS, page), 0)
    lane = lax.broadcasted_iota(jnp.int32, (Q_ROWS, page), 1)

    def group_rows(x):
        return jnp.concatenate([jnp.broadcast_to(x[g:g + 1, :], (Q_ROWS, x.shape[1]))
                                for g in range(GROUPS)], axis=0)

    def logits(q_ref, key_slabs):
        out = []
        for g in range(GROUPS):
            k = jnp.concatenate(key_slabs[g], axis=0).astype(bf)
            out.append(lax.dot_general(q_ref[0, g], k, _NT, preferred_element_type=jnp.float32))
        return jnp.concatenate(out, axis=0)

    def softmax_update(s, m_r, l_r):
        m_prev = m_r[...]
        m_new = jnp.maximum(m_prev, jnp.max(s, axis=-1, keepdims=True))
        alpha = jnp.exp(m_prev - m_new)
        p = jnp.exp(s - m_new)
        l_r[...] = alpha * l_r[...] + jnp.sum(p, axis=-1, keepdims=True)
        m_r[...] = m_new
        return alpha, p.astype(bf)

    def visit(slab, n_visit, key_term, mask):
        pages = range(n_visit)
        s = logits(qf_ref, [[slab(p, 0, g) for p in pages] for g in range(GROUPS)])
        s = s + (crow_ref[...] + group_rows(key_term))
        if mask is not None:
            s = jnp.where(mask, s, -jnp.inf)
        alpha, p_bf = softmax_update(s, mf_ref, lf_ref)
        pv = []
        for g in range(GROUPS):
            v = jnp.concatenate([slab(p, 1, g) for p in pages], axis=0).astype(bf)
            pv.append(jnp.dot(p_bf[g * Q_ROWS:(g + 1) * Q_ROWS], v, preferred_element_type=jnp.float32))
        accf_ref[...] = alpha * accf_ref[...] + jnp.concatenate(pv, axis=0)
        s = logits(qd_ref, [[slab(p, 2, g) for p in pages] for g in range(GROUPS)])
        if mask is not None:
            s = jnp.where(mask, s, -jnp.inf)
        alpha, p_bf = softmax_update(s, md_ref, ld_ref)
        pv = []
        for h in range(n_diff):
            v = jnp.concatenate(
                [jnp.concatenate([slab(p, 3, h), slab(p, 3, n_diff + h)], axis=1) for p in pages],
                axis=0).astype(bf)
            pv.append(jnp.dot(p_bf[2 * h * Q_ROWS:(2 * h + 2) * Q_ROWS], v,
                              preferred_element_type=jnp.float32))
        accd_ref[...] = alpha * accd_ref[...] + jnp.concatenate(pv, axis=0)

    @pl.when(step == 0)
    def _first():
        for r in (mf_ref, md_ref):
            r[...] = jnp.full(r.shape, -jnp.inf, jnp.float32)
        for r in (lf_ref, ld_ref, accf_ref, accd_ref, off_ref):
            r[...] = jnp.zeros(r.shape, jnp.float32)
        new_ref[...] = jnp.zeros(new_ref.shape, jnp.float32)
        for i, src in enumerate((kfn_ref, vfn_ref, kdn_ref, vdn_ref)):
            new_ref[i, 0:n_tok * GROUPS, :] = src[0]
        pref = _lane_prefix(lfn_ref[0])
        diag = jnp.concatenate([lane == row] * GROUPS, axis=0)
        crow_ref[...] = jnp.sum(jnp.where(diag, group_rows(pref), 0.0), axis=-1, keepdims=True)
        causal = jnp.concatenate([lane <= row] * GROUPS, axis=0)
        visit(lambda p, i, g: new_ref[i, pl.ds(g, page, stride=GROUPS), :], 1, -pref, causal)

    key_terms = []
    off = off_ref[...]
    for slot in range(pages_per_step):
        pref = _lane_prefix(cache[5 * slot + 2][0, 0])
        total = pref[:, page - 1:page]
        key_terms.append(total - pref + off)
        off = off + total
    off_ref[...] = off
    kv_index = (0, 1, 3, 4)
    visit(lambda p, i, g: cache[5 * p + kv_index[i]][0, 0, pl.ds(g, page, stride=GROUPS), :],
          pages_per_step, jnp.concatenate(key_terms, axis=1), None)

    @pl.when(step == pl.num_programs(1) - 1)
    def _last():
        of = accf_ref[...] / lf_ref[...]
        for g in range(GROUPS):
            fo_ref[0, :, g * HEAD_DIM:(g + 1) * HEAD_DIM] = of[g * Q_ROWS:g * Q_ROWS + n_tok]
        lam = _lambda(lq1_ref, lk1_ref, lq2_ref, lk2_ref, lam_init)
        dv = 2 * HEAD_DIM
        od = accd_ref[...] / ld_ref[...]
        for h in range(n_diff):
            o = (od[2 * h * Q_ROWS:(2 * h + 1) * Q_ROWS]
                 - lam * od[(2 * h + 1) * Q_ROWS:(2 * h + 2) * Q_ROWS])
            do_ref[0, :, h * dv:(h + 1) * dv] = _subln(o, g_ref[...], lam_init)[0:n_tok]


def decode_attention(layer, page_table, q_fox, q_diff, new_kv, new_logf_t, caches, lam_vecs, g_subln,
                     lam_init, name):
    b, n_pages = page_table.shape
    n_tok = new_kv[0].shape[1] // GROUPS
    width = GROUPS * HEAD_DIM
    page = caches[0].shape[2] // GROUPS
    assert page == LANES
    pps = PAGES_PER_STEP
    assert n_pages % pps == 0

    def per_batch(shape):
        zeros = (0,) * len(shape)
        return pl.BlockSpec((1,) + shape, lambda i, s, pt: (i,) + zeros)

    def const(shape):
        return pl.BlockSpec(shape, lambda i, s, pt: (0, 0))

    def cache_spec(arr, slot):
        def imap(i, s, pt):
            return (layer, pt[i, n_pages - 1 - (s * pps + slot)], 0, 0)
        return pl.BlockSpec((1, 1) + arr.shape[2:], imap)

    in_specs = [per_batch((GROUPS, Q_ROWS, HEAD_DIM))] * 2
    in_specs += [per_batch((n_tok * GROUPS, LANES))] * 4
    in_specs += [per_batch((GROUPS, LANES))]
    in_specs += [const((1, HEAD_DIM))] * 4 + [const((1, 2 * HEAD_DIM))]
    args = [q_fox, q_diff, *new_kv, new_logf_t, *lam_vecs, g_subln]
    for slot in range(pps):
        for arr in caches:
            in_specs.append(cache_spec(arr, slot))
            args.append(arr)
    out_spec = pl.BlockSpec((1, n_tok, width), lambda i, s, pt: (i, 0, 0))
    rows = GROUPS * Q_ROWS
    stat = pltpu.VMEM((rows, 1), jnp.float32)
    grid_spec = pltpu.PrefetchScalarGridSpec(
        num_scalar_prefetch=1,
        grid=(b, n_pages // pps),
        in_specs=in_specs,
        out_specs=[out_spec, out_spec],
        scratch_shapes=[pltpu.VMEM((4, page * GROUPS, LANES), jnp.float32),
                        stat,
                        pltpu.VMEM((GROUPS, 1), jnp.float32),
                        stat, stat, pltpu.VMEM((rows, HEAD_DIM), jnp.float32),
                        stat, stat, pltpu.VMEM((rows, 2 * HEAD_DIM), jnp.float32)])
    out = jax.ShapeDtypeStruct((b, n_tok, width), jnp.float32)
    return pl.pallas_call(
        functools.partial(_decode_kernel, n_tok=n_tok, pages_per_step=pps, lam_init=lam_init),
        grid_spec=grid_spec,
        out_shape=[out, out],
        compiler_params=_cparams(2),
        name=name,
    )(page_table, *args)


def _rope_tables(pos):
    half = ROT_DIM // 2
    inv = ROPE_THETA ** (-jnp.arange(half, dtype=jnp.float32) / half)
    ang = pos.astype(jnp.float32)[:, None] * inv[None, :]
    cos, sin = jnp.cos(ang), jnp.sin(ang)
    n = pos.shape[0]
    one = jnp.ones((n, LANES - ROT_DIM), jnp.float32)
    zero_half = jnp.zeros((n, half), jnp.float32)
    zero_rest = jnp.zeros((n, LANES - ROT_DIM), jnp.float32)
    return (jnp.concatenate([cos, cos, one], axis=1),
            jnp.concatenate([-sin, zero_half, zero_rest], axis=1),
            jnp.concatenate([zero_half, sin, zero_rest], axis=1))


def _project(h, w, rope, tag, bm):
    bf, f32 = jnp.bfloat16, jnp.float32
    mm = functools.partial(matmul, h, bm=bm, bn=1024, w_transposed=True)
    fq = mm(w["q"], epilogue="plain", out_dtypes=[bf], scale=ATTN_SCALE, name=f"{tag}_fq")
    dq = mm(w["dq"], epilogue="plain", out_dtypes=[bf], scale=ATTN_SCALE, rope_tables=rope, name=f"{tag}_dq")
    heads = functools.partial(mm, epilogue="heads", out_dtypes=[f32, bf])
    fk = heads(w["k"], head_rows=_ROWS_IDENTITY, name=f"{tag}_fk")
    fv = heads(w["v"], head_rows=_ROWS_IDENTITY, name=f"{tag}_fv")
    dk = heads(w["dk"], head_rows=_ROWS_IDENTITY, rope_tables=rope, name=f"{tag}_dk")
    dv = heads(w["dv"], head_rows=_ROWS_DIFF_V, name=f"{tag}_dv")
    gates = mm(w["g"], epilogue="sigmoid", out_dtypes=[f32], name=f"{tag}_gates")
    logf = mm(w["f"], epilogue="log_sigmoid", out_dtypes=[f32],
              extras=[(w["bf"], (1, LANES), lambda i, j: (0, 0))], name=f"{tag}_logf")
    return fq, dq, fk, fv, dk, dv, gates, logf


def _finish(x, fo, do, gates, w, tag):
    m, d = x.shape
    mix = gated_mix(fo, do, w["fox_out"], w["diff_out"], gates, name=f"{tag}_mix")
    bm = min(512, m)
    x2, h2 = matmul(mix, w["o"], epilogue="residual_norm", out_dtypes=[jnp.float32, jnp.bfloat16],
                    bm=bm, bn=d, name=f"{tag}_wo",
                    extras=[(x, (bm, d), lambda i, j: (i, 0)),
                            (w["g_mlp"], (1, d), lambda i, j: (0, 0))])
    u = matmul(h2, w["up"], epilogue="relu2", out_dtypes=[jnp.bfloat16], bm=1024, bn=1024, name=f"{tag}_up")
    return matmul(u, w["down"], epilogue="residual", out_dtypes=[jnp.float32], bm=bm, bn=512,
                  name=f"{tag}_down", extras=[(x2, (bm, 512), lambda i, j: (i, j))])


def kernel(x_prompt, x_sample, cache_fox_k, cache_fox_v, cache_fox_logf, cache_diff_k, cache_diff_v,
           page_table, g_attn, w_in, b_f, lam_q1, lam_k1, lam_q2, lam_k2, g_subln,
           w_fox_out, w_diff_out, w_o, g_mlp, w_up, w_down, g_final):
    bp, seq, d = x_prompt.shape
    bs, n_tok, _ = x_sample.shape
    depth, n_pool, page, n_fox, _ = cache_fox_k.shape
    n_diff = cache_diff_k.shape[3]
    assert n_fox == GROUPS and 2 * n_diff == GROUPS
    fox_w = n_fox * HEAD_DIM
    diff_w = n_diff * 2 * HEAD_DIM
    n_pages = page_table.shape[1]
    past_len = n_pages * page
    bf = jnp.bfloat16

    flat = (depth, n_pool, page * GROUPS, HEAD_DIM)
    dv_cache = cache_diff_v.reshape(depth, n_pool, page, n_diff, 2, HEAD_DIM)
    caches = (cache_fox_k.reshape(flat), cache_fox_v.reshape(flat),
              jnp.swapaxes(cache_fox_logf, 2, 3),
              cache_diff_k.reshape(flat),
              jnp.swapaxes(dv_cache, 3, 4).reshape(flat))

    bm_p = 1024
    rope_p = (_rope_tables(jnp.arange(seq)), seq // bm_p)
    rope_s = (_rope_tables(past_len + jnp.arange(bs * n_tok) % n_tok), 1)

    xp = x_prompt.reshape(bp * seq, d)
    xs = x_sample.reshape(bs * n_tok, d)
    w_in_t = jnp.swapaxes(w_in, 1, 2)
    outs_p = [[] for _ in range(5)]
    outs_s = [[] for _ in range(5)]

    for l in range(depth):
        lam_init = 0.8 - 0.6 * math.exp(-0.3 * l)
        o = 0
        parts = {}
        for key, size in (("q", fox_w), ("k", fox_w), ("v", fox_w), ("f", n_fox),
                          ("dq", diff_w), ("dk", diff_w), ("dv", diff_w), ("g", 2 * d)):
            parts[key] = w_in_t[l, o:o + size].astype(bf)
            o += size
        parts["f"] = jnp.pad(parts["f"], ((0, LANES - n_fox), (0, 0)))
        parts["bf"] = jnp.pad(b_f[l], (0, LANES - n_fox)).reshape(1, LANES)
        parts.update(fox_out=w_fox_out[l].astype(bf), diff_out=w_diff_out[l].astype(bf),
                     o=w_o[l].astype(bf), up=w_up[l].astype(bf), down=w_down[l].astype(bf),
                     g_mlp=g_mlp[l].reshape(1, d))
        lam_vecs = [v[l].reshape(1, HEAD_DIM) for v in (lam_q1, lam_k1, lam_q2, lam_k2)]
        gs = g_subln[l].reshape(1, 2 * HEAD_DIM)

        hp = rmsnorm(xp, g_attn[l], NORM_EPS, bf, name=f"p{l}_norm")
        fq, dq, (fk, fkb), (fv, fvb), (dk, dkb), (dv, dvb), gates, logf = _project(hp, parts, rope_p, f"p{l}", bm_p)
        ft, fnat = fox_prep(logf.reshape(bp, seq, LANES), n_fox, name=f"p{l}_foxprep")
        shp = lambda a: a.reshape(bp, seq, -1)
        fo = fox_attention(shp(fq), shp(fkb), shp(fvb), ft, fnat, n_fox, name=f"p{l}_fox")
        do = diff_attention(shp(dq), shp(dkb), shp(dvb), lam_vecs, gs, lam_init, n_diff, name=f"p{l}_diff")
        xp = _finish(xp, fo.reshape(bp * seq, fox_w), do.reshape(bp * seq, diff_w), gates, parts, f"p{l}")
        for lst, a in zip(outs_p, (fk, fv, logf[:, :n_fox], dk, dv)):
            lst.append(a)

        hs = rmsnorm(xs, g_attn[l], NORM_EPS, bf, name=f"s{l}_norm")
        fq, dq, (fk, _), (fv, _), (dk, _), (dv, _), gates, logf = _project(hs, parts, rope_s, f"s{l}", 512)

        def group_queries(q):
            q = jnp.swapaxes(q.reshape(bs, n_tok, GROUPS, HEAD_DIM), 1, 2)
            return jnp.pad(q, ((0, 0), (0, 0), (0, Q_ROWS - n_tok), (0, 0)))

        new_kv = [a.reshape(bs, n_tok * GROUPS, HEAD_DIM) for a in (fk, fv, dk, dv)]
        lf_t = jnp.swapaxes(logf[:, :n_fox].reshape(bs, n_tok, n_fox), 1, 2)
        lf_t = jnp.pad(lf_t, ((0, 0), (0, 0), (0, LANES - n_tok)))
        fo, do = decode_attention(l, page_table, group_queries(fq), group_queries(dq), new_kv, lf_t,
                                  caches, lam_vecs, gs, lam_init, name=f"s{l}_decode")
        xs = _finish(xs, fo.reshape(bs * n_tok, fox_w).astype(bf), do.reshape(bs * n_tok, diff_w).astype(bf),
                     gates, parts, f"s{l}")
        for lst, a in zip(outs_s, (fk, fv, logf[:, :n_fox], dk, dv)):
            lst.append(a)

    y_p = rmsnorm(xp, g_final, NORM_EPS, jnp.float32, name="p_final_norm").reshape(bp, seq, d)
    y_s = rmsnorm(xs, g_final, NORM_EPS, jnp.float32, name="s_final_norm").reshape(bs, n_tok, d)

    def assemble(lists, lead):
        fk, fv, lf, dk, dv = (jnp.stack(lst) for lst in lists)
        dv = jnp.swapaxes(dv.reshape((depth,) + lead + (2, n_diff, HEAD_DIM)), -3, -2)
        return (fk.reshape((depth,) + lead + (n_fox, HEAD_DIM)),
                fv.reshape((depth,) + lead + (n_fox, HEAD_DIM)),
                lf.reshape((depth,) + lead + (n_fox,)),
                dk.reshape((depth,) + lead + (n_diff, 2, HEAD_DIM)),
                dv.reshape((depth,) + lead + (n_diff, 2 * HEAD_DIM)))

    return (y_p, y_s, *assemble(outs_p, (bp, seq)), *assemble(outs_s, (bs, n_tok)))
```

```python
import functools
import math

import jax
import jax.numpy as jnp
from jax import lax
from jax.experimental import pallas as pl
from jax.experimental.pallas import tpu as pltpu

HEAD_DIM = 128
ROT_DIM = HEAD_DIM // 4
ROPE_THETA = 500000.0
ATTN_SCALE = HEAD_DIM ** -0.5
NORM_EPS = 1e-6
SUBLN_EPS = 1e-5
LANES = 128
GROUPS = 8
VMEM_LIMIT_BYTES = 56 * 1024 * 1024
PAGES_PER_STEP = 8
Q_ROWS = 16

_NT = (((1,), (1,)), ((), ()))

_ROWS_IDENTITY = tuple(range(GROUPS))
_ROWS_DIFF_V = tuple((j % 2) * 4 + j // 2 for j in range(GROUPS))


def _cparams(n_axes):
    return pltpu.CompilerParams(dimension_semantics=("arbitrary",) * n_axes,
                                vmem_limit_bytes=VMEM_LIMIT_BYTES)


def _rmsnorm_kernel(x_ref, g_ref, o_ref, *, eps):
    x = x_ref[...]
    ms = jnp.mean(x * x, axis=-1, keepdims=True)
    o_ref[...] = (x * lax.rsqrt(ms + eps) * g_ref[...]).astype(o_ref.dtype)


def rmsnorm(x, g, eps, out_dtype, name):
    m, d = x.shape
    bm = min(512, m)
    return pl.pallas_call(
        functools.partial(_rmsnorm_kernel, eps=eps),
        grid=(m // bm,),
        in_specs=[pl.BlockSpec((bm, d), lambda i: (i, 0)),
                  pl.BlockSpec((1, d), lambda i: (0, 0))],
        out_specs=pl.BlockSpec((bm, d), lambda i: (i, 0)),
        out_shape=jax.ShapeDtypeStruct((m, d), out_dtype),
        compiler_params=_cparams(1),
        name=name,
    )(x, g.reshape(1, d))


def _rope(acc, cos, sin_lo, sin_hi):
    half = ROT_DIM // 2
    chunks = []
    for c in range(acc.shape[1] // LANES):
        x = acc[:, c * LANES:(c + 1) * LANES]
        up = pltpu.roll(x, LANES - half, 1)
        dn = pltpu.roll(x, half, 1)
        chunks.append(x * cos + up * sin_lo + dn * sin_hi)
    return jnp.concatenate(chunks, axis=1)


def _mm_kernel(x_ref, w_ref, *refs, epilogue, scale, rope, w_transposed, head_rows, aliased):
    if w_transposed:
        acc = lax.dot_general(x_ref[...], w_ref[...], _NT, preferred_element_type=jnp.float32)
    else:
        acc = jnp.dot(x_ref[...], w_ref[...], preferred_element_type=jnp.float32)
    if scale != 1.0:
        acc = acc * scale
    if rope:
        acc = _rope(acc, refs[0][...], refs[1][...], refs[2][...])
        refs = refs[3:]
    if aliased:
        refs = refs[1:]
    if epilogue == "plain":
        for o_ref in refs:
            o_ref[...] = acc.astype(o_ref.dtype)
    elif epilogue == "heads":
        hm_ref, o_ref = refs
        bm = acc.shape[0]
        for j, r in enumerate(head_rows):
            hm_ref[pl.ds(r, bm, stride=GROUPS), :] = acc[:, j * LANES:(j + 1) * LANES]
        o_ref[...] = acc.astype(o_ref.dtype)
    elif epilogue == "sigmoid":
        (o_ref,) = refs
        o_ref[...] = jax.nn.sigmoid(acc).astype(o_ref.dtype)
    elif epilogue == "log_sigmoid":
        b_ref, o_ref = refs
        o_ref[...] = jax.nn.log_sigmoid(acc + b_ref[...]).astype(o_ref.dtype)
    elif epilogue == "relu2":
        (o_ref,) = refs
        r = jnp.maximum(acc, 0.0)
        o_ref[...] = (r * r).astype(o_ref.dtype)
    elif epilogue == "residual":
        res_ref, o_ref = refs
        o_ref[...] = res_ref[...] + acc
    elif epilogue == "residual_norm":
        res_ref, g_ref, x_ref_out, h_ref_out = refs
        xn = res_ref[...] + acc
        x_ref_out[...] = xn
        ms = jnp.mean(xn * xn, axis=-1, keepdims=True)
        h_ref_out[...] = (xn * lax.rsqrt(ms + NORM_EPS) * g_ref[...]).astype(h_ref_out.dtype)
    else:
        raise ValueError(epilogue)


def matmul(x, w, *, epilogue, out_dtypes, name, bm, bn, scale=1.0, extras=(), rope_tables=None,
           w_transposed=False, head_rows=None, stack=None):
    m, k = x.shape
    n = w.shape[0] if w_transposed else w.shape[1]
    bm, bn = min(bm, m), min(bn, n)
    assert m % bm == 0 and n % bn == 0
    in_specs = [pl.BlockSpec((bm, k), lambda i, j: (i, 0)),
                pl.BlockSpec((bn, k), lambda i, j: (j, 0)) if w_transposed
                else pl.BlockSpec((k, bn), lambda i, j: (0, j))]
    args = [x, w]
    if rope_tables is not None:
        tables, n_blocks = rope_tables
        for t in tables:
            in_specs.append(pl.BlockSpec((bm, LANES), lambda i, j: (i % n_blocks, 0)))
            args.append(t)
    for arr, blk, imap in extras:
        in_specs.append(pl.BlockSpec(blk, imap))
        args.append(arr)
    aliases = {}
    if epilogue == "heads":
        assert bn == GROUPS * LANES == n
        slot, n_slots, previous = stack if stack is not None else (0, 1, None)
        first_block = slot * (m // bm)
        if previous is not None:
            in_specs.append(pl.BlockSpec(memory_space=pl.ANY))
            args.append(previous)
            aliases = {len(args) - 1: 0}
        out_specs = [pl.BlockSpec((bm * GROUPS, LANES), lambda i, j: (first_block + i, 0)),
                     pl.BlockSpec((bm, bn), lambda i, j: (i, j))]
        out_shape = [jax.ShapeDtypeStruct((n_slots * m * GROUPS, LANES), out_dtypes[0]),
                     jax.ShapeDtypeStruct((m, n), out_dtypes[1])]
    else:
        out_specs = [pl.BlockSpec((bm, bn), lambda i, j: (i, j)) for _ in out_dtypes]
        out_shape = [jax.ShapeDtypeStruct((m, n), dt) for dt in out_dtypes]
    outs = pl.pallas_call(
        functools.partial(_mm_kernel, epilogue=epilogue, scale=scale, rope=rope_tables is not None,
                          w_transposed=w_transposed, head_rows=head_rows, aliased=bool(aliases)),
        grid=(m // bm, n // bn),
        in_specs=in_specs, out_specs=out_specs, out_shape=out_shape,
        input_output_aliases=aliases,
        compiler_params=_cparams(2),
        name=name,
    )(*args)
    return outs[0] if len(outs) == 1 else outs


def _mix_kernel(fo_ref, do_ref, wf_ref, wd_ref, g0_ref, g1_ref, o_ref):
    a = jnp.dot(fo_ref[...], wf_ref[...], preferred_element_type=jnp.float32)
    b = jnp.dot(do_ref[...], wd_ref[...], preferred_element_type=jnp.float32)
    o_ref[...] = (g0_ref[...] * a + g1_ref[...] * b).astype(o_ref.dtype)


def gated_mix(fo, do, w_fox_out, w_diff_out, gates, name):
    m, kf = fo.shape
    kd = do.shape[1]
    d = w_fox_out.shape[1]
    bm, bn = min(1024, m), 512
    nj = d // bn
    return pl.pallas_call(
        _mix_kernel,
        grid=(m // bm, nj),
        in_specs=[pl.BlockSpec((bm, kf), lambda i, j: (i, 0)),
                  pl.BlockSpec((bm, kd), lambda i, j: (i, 0)),
                  pl.BlockSpec((kf, bn), lambda i, j: (0, j)),
                  pl.BlockSpec((kd, bn), lambda i, j: (0, j)),
                  pl.BlockSpec((bm, bn), lambda i, j: (i, j)),
                  pl.BlockSpec((bm, bn), lambda i, j: (i, j + nj))],
        out_specs=pl.BlockSpec((bm, bn), lambda i, j: (i, j)),
        out_shape=jax.ShapeDtypeStruct((m, d), jnp.bfloat16),
        compiler_params=_cparams(2),
        name=name,
    )(fo, do, w_fox_out, w_diff_out, gates, gates)


def _lane_prefix(x):
    lane = lax.broadcasted_iota(jnp.int32, x.shape, 1)
    shift = 1
    while shift < LANES:
        x = x + jnp.where(lane >= shift, pltpu.roll(x, shift, 1), 0.0)
        shift *= 2
    return x


def _fox_prep_kernel(lf_ref, ft_ref, fn_ref, *, n_heads):
    seq = lf_ref.shape[1]
    lt = lf_ref[0].T[:n_heads, :]
    blocks = [_lane_prefix(lt[:, c * LANES:(c + 1) * LANES]) for c in range(seq // LANES)]
    carry = jnp.zeros((n_heads, 1), jnp.float32)
    outs = []
    for blk in blocks:
        blk = blk + carry
        carry = blk[:, LANES - 1:LANES]
        outs.append(blk)
    ft = jnp.concatenate(outs, axis=1)
    ft_ref[0] = ft
    padded = jnp.concatenate([ft, jnp.zeros((LANES - n_heads, seq), jnp.float32)], axis=0)
    fn_ref[0] = padded.T


def fox_prep(logf_pad, n_heads, name):
    b, seq, _ = logf_pad.shape
    return pl.pallas_call(
        functools.partial(_fox_prep_kernel, n_heads=n_heads),
        grid=(b,),
        in_specs=[pl.BlockSpec((1, seq, LANES), lambda i: (i, 0, 0))],
        out_specs=[pl.BlockSpec((1, n_heads, seq), lambda i: (i, 0, 0)),
                   pl.BlockSpec((1, seq, LANES), lambda i: (i, 0, 0))],
        out_shape=[jax.ShapeDtypeStruct((b, n_heads, seq), jnp.float32),
                   jax.ShapeDtypeStruct((b, seq, LANES), jnp.float32)],
        compiler_params=_cparams(1),
        name=name,
    )(logf_pad)


ROW_CHUNK = 64
COL_CHUNK = 256


def _softmax_block(s_ref, p_ref, m_ref, l_ref, acc_ref, v, key_term=None, row_shift=None, lead=None):
    tq, tk = s_ref.shape
    col_minus_row = (lax.broadcasted_iota(jnp.int32, (ROW_CHUNK, COL_CHUNK), 1)
                     - lax.broadcasted_iota(jnp.int32, (ROW_CHUNK, COL_CHUNK), 0))
    for r in range(tq // ROW_CHUNK):
        r0 = r * ROW_CHUNK
        rs = slice(r0, r0 + ROW_CHUNK)

        def piece(c, r0=r0, rs=rs):
            c0 = c * COL_CHUNK
            t = s_ref[rs, c0:c0 + COL_CHUNK]
            if key_term is not None:
                t = t - key_term[:, c0:c0 + COL_CHUNK]
            if lead is not None:
                t = jnp.where(col_minus_row <= lead - (c0 - r0), t, -jnp.inf)
            return t

        n_col = tk // COL_CHUNK
        mx = piece(0)
        for c in range(1, n_col):
            mx = jnp.maximum(mx, piece(c))
        t_max = jnp.max(mx, axis=-1, keepdims=True)
        m_prev = m_ref[rs, :]
        if row_shift is not None:
            shift_r = row_shift[rs, :]
            m_new = jnp.maximum(m_prev, t_max + shift_r)
            sub = m_new - shift_r
        else:
            m_new = jnp.maximum(m_prev, t_max)
            sub = m_new
        alpha = jnp.exp(m_prev - m_new)
        lsum = None
        for c in range(n_col):
            p = jnp.exp(piece(c) - sub)
            lsum = p if lsum is None else lsum + p
            p_ref[rs, c * COL_CHUNK:(c + 1) * COL_CHUNK] = p.astype(p_ref.dtype)
        l_ref[rs, :] = alpha * l_ref[rs, :] + jnp.sum(lsum, axis=-1, keepdims=True)
        acc_ref[rs, :] = alpha * acc_ref[rs, :]
        m_ref[rs, :] = m_new
    acc_ref[...] += jnp.dot(p_ref[...], v, preferred_element_type=jnp.float32)


def _causal_blocks(qi, tq, tk):
    assert tk % tq == 0
    n_full = qi // (tk // tq)
    return n_full, qi * tq - n_full * tk


def _fox_attn_kernel(q_ref, k_ref, v_ref, ft_ref, fn_ref, o_ref, m_ref, l_ref, acc_ref, s_ref, p_ref,
                     *, tq, tk, n_heads):
    n_full, last_lead = _causal_blocks(pl.program_id(1), tq, tk)
    m_ref[...] = jnp.full(m_ref.shape, -jnp.inf, jnp.float32)
    l_ref[...] = jnp.zeros(l_ref.shape, jnp.float32)
    acc_ref[...] = jnp.zeros(acc_ref.shape, jnp.float32)

    def block(start, lead):
        for h in range(n_heads):
            sl = slice(h * HEAD_DIM, (h + 1) * HEAD_DIM)
            k = k_ref[0, pl.ds(start, tk), sl]
            s_ref[h] = lax.dot_general(q_ref[0, :, sl], k, _NT, preferred_element_type=jnp.float32)
            fk = ft_ref[0, h:h + 1, pl.ds(start, tk)]
            fq = fn_ref[0, :, h:h + 1]
            _softmax_block(s_ref.at[h], p_ref.at[h], m_ref.at[h], l_ref.at[h], acc_ref.at[h],
                           v_ref[0, pl.ds(start, tk), sl], key_term=fk, row_shift=fq, lead=lead)

    def body(kj, carry):
        block(pl.multiple_of(kj * tk, tk), None)
        return carry

    lax.fori_loop(0, n_full, body, 0)
    block(pl.multiple_of(n_full * tk, tk), last_lead)
    for h in range(n_heads):
        o_ref[0, :, h * HEAD_DIM:(h + 1) * HEAD_DIM] = (acc_ref[h] / l_ref[h]).astype(o_ref.dtype)


def fox_attention(q, k, v, ft, fnat, n_heads, name, tq=256, tk=1024):
    b, seq, width = q.shape
    return pl.pallas_call(
        functools.partial(_fox_attn_kernel, tq=tq, tk=tk, n_heads=n_heads),
        grid=(b, seq // tq),
        in_specs=[pl.BlockSpec((1, tq, width), lambda i, j: (i, j, 0)),
                  pl.BlockSpec((1, seq, width), lambda i, j: (i, 0, 0)),
                  pl.BlockSpec((1, seq, width), lambda i, j: (i, 0, 0)),
                  pl.BlockSpec((1, n_heads, seq), lambda i, j: (i, 0, 0)),
                  pl.BlockSpec((1, tq, LANES), lambda i, j: (i, j, 0))],
        out_specs=pl.BlockSpec((1, tq, width), lambda i, j: (i, j, 0)),
        out_shape=jax.ShapeDtypeStruct((b, seq, width), jnp.bfloat16),
        scratch_shapes=[pltpu.VMEM((n_heads, tq, 1), jnp.float32),
                        pltpu.VMEM((n_heads, tq, 1), jnp.float32),
                        pltpu.VMEM((n_heads, tq, HEAD_DIM), jnp.float32),
                        pltpu.VMEM((n_heads, tq, tk), jnp.float32),
                        pltpu.VMEM((n_heads, tq, tk), jnp.bfloat16)],
        compiler_params=_cparams(2),
        name=name,
    )(q, k, v, ft, fnat)


def _lambda(lq1_ref, lk1_ref, lq2_ref, lk2_ref, lam_init):
    d1 = jnp.sum(lq1_ref[...] * lk1_ref[...], axis=-1, keepdims=True)
    d2 = jnp.sum(lq2_ref[...] * lk2_ref[...], axis=-1, keepdims=True)
    return jnp.exp(d1) - jnp.exp(d2) + lam_init


def _subln(o, g, lam_init):
    ms = jnp.mean(o * o, axis=-1, keepdims=True)
    return o * lax.rsqrt(ms + SUBLN_EPS) * g * (1.0 - lam_init)


def _diff_attn_kernel(q_ref, k_ref, v_ref, lq1_ref, lk1_ref, lq2_ref, lk2_ref, g_ref, o_ref,
                      m_ref, l_ref, acc_ref, s_ref, p_ref, *, tq, tk, n_heads, lam_init):
    n_full, last_lead = _causal_blocks(pl.program_id(1), tq, tk)
    lam = _lambda(lq1_ref, lk1_ref, lq2_ref, lk2_ref, lam_init)
    dv = 2 * HEAD_DIM
    m_ref[...] = jnp.full(m_ref.shape, -jnp.inf, jnp.float32)
    l_ref[...] = jnp.zeros(l_ref.shape, jnp.float32)
    acc_ref[...] = jnp.zeros(acc_ref.shape, jnp.float32)

    def block(start, lead):
        for g in range(2 * n_heads):
            sl = slice(g * HEAD_DIM, (g + 1) * HEAD_DIM)
            k = k_ref[0, pl.ds(start, tk), sl]
            s_ref[g] = lax.dot_general(q_ref[0, :, sl], k, _NT, preferred_element_type=jnp.float32)
            v = v_ref[0, pl.ds(start, tk), (g // 2) * dv:(g // 2 + 1) * dv]
            _softmax_block(s_ref.at[g], p_ref.at[g], m_ref.at[g], l_ref.at[g], acc_ref.at[g], v, lead=lead)

    def body(kj, carry):
        block(pl.multiple_of(kj * tk, tk), None)
        return carry

    lax.fori_loop(0, n_full, body, 0)
    block(pl.multiple_of(n_full * tk, tk), last_lead)
    for h in range(n_heads):
        o = acc_ref[2 * h] / l_ref[2 * h] - lam * (acc_ref[2 * h + 1] / l_ref[2 * h + 1])
        o_ref[0, :, h * dv:(h + 1) * dv] = _subln(o, g_ref[...], lam_init).astype(o_ref.dtype)


def diff_attention(q, k, v, lam_vecs, g_subln, lam_init, n_heads, name, tq=256, tk=1024):
    b, seq, width = q.shape
    vec = pl.BlockSpec((1, HEAD_DIM), lambda i, j: (0, 0))
    return pl.pallas_call(
        functools.partial(_diff_attn_kernel, tq=tq, tk=tk, n_heads=n_heads, lam_init=lam_init),
        grid=(b, seq // tq),
        in_specs=[pl.BlockSpec((1, tq, width), lambda i, j: (i, j, 0)),
                  pl.BlockSpec((1, seq, width), lambda i, j: (i, 0, 0)),
                  pl.BlockSpec((1, seq, width), lambda i, j: (i, 0, 0)),
                  vec, vec, vec, vec,
                  pl.BlockSpec((1, 2 * HEAD_DIM), lambda i, j: (0, 0))],
        out_specs=pl.BlockSpec((1, tq, width), lambda i, j: (i, j, 0)),
        out_shape=jax.ShapeDtypeStruct((b, seq, width), jnp.bfloat16),
        scratch_shapes=[pltpu.VMEM((2 * n_heads, tq, 1), jnp.float32),
                        pltpu.VMEM((2 * n_heads, tq, 1), jnp.float32),
                        pltpu.VMEM((2 * n_heads, tq, 2 * HEAD_DIM), jnp.float32),
                        pltpu.VMEM((2 * n_heads, tq, tk), jnp.float32),
                        pltpu.VMEM((2 * n_heads, tq, tk), jnp.bfloat16)],
        compiler_params=_cparams(2),
        name=name,
    )(q, k, v, *lam_vecs, g_subln)


def _decode_kernel(pt_ref, qf_ref, qd_ref, kfn_ref, vfn_ref, kdn_ref, vdn_ref, lfn_ref,
                   lq1_ref, lk1_ref, lq2_ref, lk2_ref, g_ref, *rest,
                   n_tok, pages_per_step, lam_init):
    n_cache = 5 * pages_per_step
    cache = rest[:n_cache]
    fo_ref, do_ref = rest[n_cache:n_cache + 2]
    (new_ref, crow_ref, off_ref, mf_ref, lf_ref, accf_ref, md_ref, ld_ref, accd_ref) = rest[n_cache + 2:]
    step = pl.program_id(1)
    page = LANES
    n_diff = GROUPS // 2
    bf = jnp.bfloat16
    row = lax.broadcasted_iota(jnp.int32, (Q_ROWS, page), 0)
    lane = lax.broadcasted_iota(jnp.int32, (Q_ROWS, page), 1)

    def group_rows(x):
        return jnp.concatenate([jnp.broadcast_to(x[g:g + 1, :], (Q_ROWS, x.shape[1]))
                                for g in range(GROUPS)], axis=0)

    def logits(q_ref, key_slabs):
        out = []
        for g in range(GROUPS):
            k = jnp.concatenate(key_slabs[g], axis=0).astype(bf)
            out.append(lax.dot_general(q_ref[0, g], k, _NT, preferred_element_type=jnp.float32))
        return jnp.concatenate(out, axis=0)

    def softmax_update(s, m_r, l_r):
        m_prev = m_r[...]
        m_new = jnp.maximum(m_prev, jnp.max(s, axis=-1, keepdims=True))
        alpha = jnp.exp(m_prev - m_new)
        p = jnp.exp(s - m_new)
        l_r[...] = alpha * l_r[...] + jnp.sum(p, axis=-1, keepdims=True)
        m_r[...] = m_new
        return alpha, p.astype(bf)

    def visit(slab, n_visit, key_term, mask):
        pages = range(n_visit)
        s = logits(qf_ref, [[slab(p, 0, g) for p in pages] for g in range(GROUPS)])
        s = s + (crow_ref[...] + group_rows(key_term))
        if mask is not None:
            s = jnp.where(mask, s, -jnp.inf)
        alpha, p_bf = softmax_update(s, mf_ref, lf_ref)
        pv = []
        for g in range(GROUPS):
            v = jnp.concatenate([slab(p, 1, g) for p in pages], axis=0).astype(bf)
            pv.append(jnp.dot(p_bf[g * Q_ROWS:(g + 1) * Q_ROWS], v, preferred_element_type=jnp.float32))
        accf_ref[...] = alpha * accf_ref[...] + jnp.concatenate(pv, axis=0)
        s = logits(qd_ref, [[slab(p, 2, g) for p in pages] for g in range(GROUPS)])
        if mask is not None:
            s = jnp.where(mask, s, -jnp.inf)
        alpha, p_bf = softmax_update(s, md_ref, ld_ref)
        pv = []
        for h in range(n_diff):
            v = jnp.concatenate(
                [jnp.concatenate([slab(p, 3, h), slab(p, 3, n_diff + h)], axis=1) for p in pages],
                axis=0).astype(bf)
            pv.append(jnp.dot(p_bf[2 * h * Q_ROWS:(2 * h + 2) * Q_ROWS], v,
                              preferred_element_type=jnp.float32))
        accd_ref[...] = alpha * accd_ref[...] + jnp.concatenate(pv, axis=0)

    @pl.when(step == 0)
    def _first():
        for r in (mf_ref, md_ref):
            r[...] = jnp.full(r.shape, -jnp.inf, jnp.float32)
        for r in (lf_ref, ld_ref, accf_ref, accd_ref, off_ref):
            r[...] = jnp.zeros(r.shape, jnp.float32)
        new_ref[...] = jnp.zeros(new_ref.shape, jnp.float32)
        for i, src in enumerate((kfn_ref, vfn_ref, kdn_ref, vdn_ref)):
            new_ref[i, 0:n_tok * GROUPS, :] = src[0]
        pref = _lane_prefix(lfn_ref[0])
        diag = jnp.concatenate([lane == row] * GROUPS, axis=0)
        crow_ref[...] = jnp.sum(jnp.where(diag, group_rows(pref), 0.0), axis=-1, keepdims=True)
        causal = jnp.concatenate([lane <= row] * GROUPS, axis=0)
        visit(lambda p, i, g: new_ref[i, pl.ds(g, page, stride=GROUPS), :], 1, -pref, causal)

    key_terms = []
    off = off_ref[...]
    for slot in range(pages_per_step):
        pref = _lane_prefix(cache[5 * slot + 2][0, 0])
        total = pref[:, page - 1:page]
        key_terms.append(total - pref + off)
        off = off + total
    off_ref[...] = off
    kv_index = (0, 1, 3, 4)
    visit(lambda p, i, g: cache[5 * p + kv_index[i]][0, 0, pl.ds(g, page, stride=GROUPS), :],
          pages_per_step, jnp.concatenate(key_terms, axis=1), None)

    @pl.when(step == pl.num_programs(1) - 1)
    def _last():
        of = accf_ref[...] / lf_ref[...]
        for g in range(GROUPS):
            fo_ref[0, :, g * HEAD_DIM:(g + 1) * HEAD_DIM] = of[g * Q_ROWS:g * Q_ROWS + n_tok]
        lam = _lambda(lq1_ref, lk1_ref, lq2_ref, lk2_ref, lam_init)
        dv = 2 * HEAD_DIM
        od = accd_ref[...] / ld_ref[...]
        for h in range(n_diff):
            o = (od[2 * h * Q_ROWS:(2 * h + 1) * Q_ROWS]
                 - lam * od[(2 * h + 1) * Q_ROWS:(2 * h + 2) * Q_ROWS])
            do_ref[0, :, h * dv:(h + 1) * dv] = _subln(o, g_ref[...], lam_init)[0:n_tok]


def decode_attention(layer, page_table, q_fox, q_diff, new_kv, new_logf_t, caches, lam_vecs, g_subln,
                     lam_init, name):
    b, n_pages = page_table.shape
    n_tok = new_kv[0].shape[1] // GROUPS
    width = GROUPS * HEAD_DIM
    page = caches[0].shape[2] // GROUPS
    assert page == LANES
    pps = PAGES_PER_STEP
    assert n_pages % pps == 0

    def per_batch(shape):
        zeros = (0,) * len(shape)
        return pl.BlockSpec((1,) + shape, lambda i, s, pt: (i,) + zeros)

    def const(shape):
        return pl.BlockSpec(shape, lambda i, s, pt: (0, 0))

    def cache_spec(arr, slot):
        def imap(i, s, pt):
            return (layer, pt[i, n_pages - 1 - (s * pps + slot)], 0, 0)
        return pl.BlockSpec((1, 1) + arr.shape[2:], imap)

    in_specs = [per_batch((GROUPS, Q_ROWS, HEAD_DIM))] * 2
    in_specs += [per_batch((n_tok * GROUPS, LANES))] * 4
    in_specs += [per_batch((GROUPS, LANES))]
    in_specs += [const((1, HEAD_DIM))] * 4 + [const((1, 2 * HEAD_DIM))]
    args = [q_fox, q_diff, *new_kv, new_logf_t, *lam_vecs, g_subln]
    for slot in range(pps):
        for arr in caches:
            in_specs.append(cache_spec(arr, slot))
            args.append(arr)
    out_spec = pl.BlockSpec((1, n_tok, width), lambda i, s, pt: (i, 0, 0))
    rows = GROUPS * Q_ROWS
    stat = pltpu.VMEM((rows, 1), jnp.float32)
    grid_spec = pltpu.PrefetchScalarGridSpec(
        num_scalar_prefetch=1,
        grid=(b, n_pages // pps),
        in_specs=in_specs,
        out_specs=[out_spec, out_spec],
        scratch_shapes=[pltpu.VMEM((4, page * GROUPS, LANES), jnp.float32),
                        stat,
                        pltpu.VMEM((GROUPS, 1), jnp.float32),
                        stat, stat, pltpu.VMEM((rows, HEAD_DIM), jnp.float32),
                        stat, stat, pltpu.VMEM((rows, 2 * HEAD_DIM), jnp.float32)])
    out = jax.ShapeDtypeStruct((b, n_tok, width), jnp.float32)
    return pl.pallas_call(
        functools.partial(_decode_kernel, n_tok=n_tok, pages_per_step=pps, lam_init=lam_init),
        grid_spec=grid_spec,
        out_shape=[out, out],
        compiler_params=_cparams(2),
        name=name,
    )(page_table, *args)


def _rope_tables(pos):
    half = ROT_DIM // 2
    inv = ROPE_THETA ** (-jnp.arange(half, dtype=jnp.float32) / half)
    ang = pos.astype(jnp.float32)[:, None] * inv[None, :]
    cos, sin = jnp.cos(ang), jnp.sin(ang)
    n = pos.shape[0]
    one = jnp.ones((n, LANES - ROT_DIM), jnp.float32)
    zero_half = jnp.zeros((n, half), jnp.float32)
    zero_rest = jnp.zeros((n, LANES - ROT_DIM), jnp.float32)
    return (jnp.concatenate([cos, cos, one], axis=1),
            jnp.concatenate([-sin, zero_half, zero_rest], axis=1),
            jnp.concatenate([zero_half, sin, zero_rest], axis=1))


def _project(h, w, rope, tag, bm, stacks=None):
    bf, f32 = jnp.bfloat16, jnp.float32
    stacks = stacks or {}
    mm = functools.partial(matmul, h, bm=bm, bn=1024, w_transposed=True)
    fq = mm(w["q"], epilogue="plain", out_dtypes=[bf], scale=ATTN_SCALE, name=f"{tag}_fq")
    dq = mm(w["dq"], epilogue="plain", out_dtypes=[bf], scale=ATTN_SCALE, rope_tables=rope, name=f"{tag}_dq")
    heads = functools.partial(mm, epilogue="heads", out_dtypes=[f32, bf])
    fk = heads(w["k"], head_rows=_ROWS_IDENTITY, stack=stacks.get("k"), name=f"{tag}_fk")
    fv = heads(w["v"], head_rows=_ROWS_IDENTITY, stack=stacks.get("v"), name=f"{tag}_fv")
    dk = heads(w["dk"], head_rows=_ROWS_IDENTITY, rope_tables=rope, stack=stacks.get("dk"), name=f"{tag}_dk")
    dv = heads(w["dv"], head_rows=_ROWS_DIFF_V, stack=stacks.get("dv"), name=f"{tag}_dv")
    gates = mm(w["g"], epilogue="sigmoid", out_dtypes=[f32], name=f"{tag}_gates")
    logf = mm(w["f"], epilogue="log_sigmoid", out_dtypes=[f32],
              extras=[(w["bf"], (1, LANES), lambda i, j: (0, 0))], name=f"{tag}_logf")
    return fq, dq, fk, fv, dk, dv, gates, logf


def _finish(x, fo, do, gates, w, tag):
    m, d = x.shape
    mix = gated_mix(fo, do, w["fox_out"], w["diff_out"], gates, name=f"{tag}_mix")
    bm = min(512, m)
    x2, h2 = matmul(mix, w["o"], epilogue="residual_norm", out_dtypes=[jnp.float32, jnp.bfloat16],
                    bm=bm, bn=d, name=f"{tag}_wo",
                    extras=[(x, (bm, d), lambda i, j: (i, 0)),
                            (w["g_mlp"], (1, d), lambda i, j: (0, 0))])
    u = matmul(h2, w["up"], epilogue="relu2", out_dtypes=[jnp.bfloat16], bm=1024, bn=1024, name=f"{tag}_up")
    return matmul(u, w["down"], epilogue="residual", out_dtypes=[jnp.float32], bm=bm, bn=512,
                  name=f"{tag}_down", extras=[(x2, (bm, 512), lambda i, j: (i, j))])


def kernel(x_prompt, x_sample, cache_fox_k, cache_fox_v, cache_fox_logf, cache_diff_k, cache_diff_v,
           page_table, g_attn, w_in, b_f, lam_q1, lam_k1, lam_q2, lam_k2, g_subln,
           w_fox_out, w_diff_out, w_o, g_mlp, w_up, w_down, g_final):
    bp, seq, d = x_prompt.shape
    bs, n_tok, _ = x_sample.shape
    depth, n_pool, page, n_fox, _ = cache_fox_k.shape
    n_diff = cache_diff_k.shape[3]
    assert n_fox == GROUPS and 2 * n_diff == GROUPS
    fox_w = n_fox * HEAD_DIM
    diff_w = n_diff * 2 * HEAD_DIM
    n_pages = page_table.shape[1]
    past_len = n_pages * page
    bf = jnp.bfloat16

    flat = (depth, n_pool, page * GROUPS, HEAD_DIM)
    dv_cache = cache_diff_v.reshape(depth, n_pool, page, n_diff, 2, HEAD_DIM)
    caches = (cache_fox_k.reshape(flat), cache_fox_v.reshape(flat),
              jnp.swapaxes(cache_fox_logf, 2, 3),
              cache_diff_k.reshape(flat),
              jnp.swapaxes(dv_cache, 3, 4).reshape(flat))

    bm_p = 1024
    rope_p = (_rope_tables(jnp.arange(seq)), seq // bm_p)
    rope_s = (_rope_tables(past_len + jnp.arange(bs * n_tok) % n_tok), 1)

    xp = x_prompt.reshape(bp * seq, d)
    xs = x_sample.reshape(bs * n_tok, d)
    w_in_t = jnp.swapaxes(w_in, 1, 2)
    stacked_p = {}
    logf_p = []
    outs_s = [[] for _ in range(5)]

    for l in range(depth):
        lam_init = 0.8 - 0.6 * math.exp(-0.3 * l)
        o = 0
        parts = {}
        for key, size in (("q", fox_w), ("k", fox_w), ("v", fox_w), ("f", n_fox),
                          ("dq", diff_w), ("dk", diff_w), ("dv", diff_w), ("g", 2 * d)):
            parts[key] = w_in_t[l, o:o + size].astype(bf)
            o += size
        parts["f"] = jnp.pad(parts["f"], ((0, LANES - n_fox), (0, 0)))
        parts["bf"] = jnp.pad(b_f[l], (0, LANES - n_fox)).reshape(1, LANES)
        parts.update(fox_out=w_fox_out[l].astype(bf), diff_out=w_diff_out[l].astype(bf),
                     o=w_o[l].astype(bf), up=w_up[l].astype(bf), down=w_down[l].astype(bf),
                     g_mlp=g_mlp[l].reshape(1, d))
        lam_vecs = [v[l].reshape(1, HEAD_DIM) for v in (lam_q1, lam_k1, lam_q2, lam_k2)]
        gs = g_subln[l].reshape(1, 2 * HEAD_DIM)

        hp = rmsnorm(xp, g_attn[l], NORM_EPS, bf, name=f"p{l}_norm")
        stacks = {key: (l, depth, stacked_p.get(key)) for key in ("k", "v", "dk", "dv")}
        fq, dq, (fk, fkb), (fv, fvb), (dk, dkb), (dv, dvb), gates, logf = _project(
            hp, parts, rope_p, f"p{l}", bm_p, stacks)
        stacked_p.update(k=fk, v=fv, dk=dk, dv=dv)
        ft, fnat = fox_prep(logf.reshape(bp, seq, LANES), n_fox, name=f"p{l}_foxprep")
        shp = lambda a: a.reshape(bp, seq, -1)
        fo = fox_attention(shp(fq), shp(fkb), shp(fvb), ft, fnat, n_fox, name=f"p{l}_fox")
        do = diff_attention(shp(dq), shp(dkb), shp(dvb), lam_vecs, gs, lam_init, n_diff, name=f"p{l}_diff")
        xp = _finish(xp, fo.reshape(bp * seq, fox_w), do.reshape(bp * seq, diff_w), gates, parts, f"p{l}")
        logf_p.append(logf[:, :n_fox])

        hs = rmsnorm(xs, g_attn[l], NORM_EPS, bf, name=f"s{l}_norm")
        fq, dq, (fk, _), (fv, _), (dk, _), (dv, _), gates, logf = _project(hs, parts, rope_s, f"s{l}", 512)

        def group_queries(q):
            q = jnp.swapaxes(q.reshape(bs, n_tok, GROUPS, HEAD_DIM), 1, 2)
            return jnp.pad(q, ((0, 0), (0, 0), (0, Q_ROWS - n_tok), (0, 0)))

        new_kv = [a.reshape(bs, n_tok * GROUPS, HEAD_DIM) for a in (fk, fv, dk, dv)]
        lf_t = jnp.swapaxes(logf[:, :n_fox].reshape(bs, n_tok, n_fox), 1, 2)
        lf_t = jnp.pad(lf_t, ((0, 0), (0, 0), (0, LANES - n_tok)))
        fo, do = decode_attention(l, page_table, group_queries(fq), group_queries(dq), new_kv, lf_t,
                                  caches, lam_vecs, gs, lam_init, name=f"s{l}_decode")
        xs = _finish(xs, fo.reshape(bs * n_tok, fox_w).astype(bf), do.reshape(bs * n_tok, diff_w).astype(bf),
                     gates, parts, f"s{l}")
        for lst, a in zip(outs_s, (fk, fv, logf[:, :n_fox], dk, dv)):
            lst.append(a)

    y_p = rmsnorm(xp, g_final, NORM_EPS, jnp.float32, name="p_final_norm").reshape(bp, seq, d)
    y_s = rmsnorm(xs, g_final, NORM_EPS, jnp.float32, name="s_final_norm").reshape(bs, n_tok, d)

    def assemble(arrays, lead):
        fk, fv, lf, dk, dv = arrays
        dv = jnp.swapaxes(dv.reshape((depth,) + lead + (2, n_diff, HEAD_DIM)), -3, -2)
        return (fk.reshape((depth,) + lead + (n_fox, HEAD_DIM)),
                fv.reshape((depth,) + lead + (n_fox, HEAD_DIM)),
                lf.reshape((depth,) + lead + (n_fox,)),
                dk.reshape((depth,) + lead + (n_diff, 2, HEAD_DIM)),
                dv.reshape((depth,) + lead + (n_diff, 2 * HEAD_DIM)))

    prompt = (stacked_p["k"], stacked_p["v"], jnp.stack(logf_p), stacked_p["dk"], stacked_p["dv"])
    sample = [jnp.stack(lst) for lst in outs_s]
    return (y_p, y_s, *assemble(prompt, (bp, seq)), *assemble(sample, (bs, n_tok)))
```

```python
import functools
import math

import jax
import jax.numpy as jnp
from jax import lax
from jax.experimental import pallas as pl
from jax.experimental.pallas import tpu as pltpu

HEAD_DIM = 128
ROT_DIM = HEAD_DIM // 4
ROPE_THETA = 500000.0
ATTN_SCALE = HEAD_DIM ** -0.5
NORM_EPS = 1e-6
SUBLN_EPS = 1e-5
LANES = 128
GROUPS = 8
VMEM_LIMIT_BYTES = 56 * 1024 * 1024
PAGES_PER_STEP = 8
Q_ROWS = 16

_NT = (((1,), (1,)), ((), ()))

_ROWS_IDENTITY = tuple(range(GROUPS))
_ROWS_DIFF_V = tuple((j % 2) * 4 + j // 2 for j in range(GROUPS))


def _cparams(n_axes):
    return pltpu.CompilerParams(dimension_semantics=("arbitrary",) * n_axes,
                                vmem_limit_bytes=VMEM_LIMIT_BYTES)


def _rmsnorm_kernel(x_ref, g_ref, o_ref, *, eps):
    x = x_ref[...]
    ms = jnp.mean(x * x, axis=-1, keepdims=True)
    o_ref[...] = (x * lax.rsqrt(ms + eps) * g_ref[...]).astype(o_ref.dtype)


def rmsnorm(x, g, eps, out_dtype, name):
    m, d = x.shape
    bm = min(512, m)
    return pl.pallas_call(
        functools.partial(_rmsnorm_kernel, eps=eps),
        grid=(m // bm,),
        in_specs=[pl.BlockSpec((bm, d), lambda i: (i, 0)),
                  pl.BlockSpec((1, d), lambda i: (0, 0))],
        out_specs=pl.BlockSpec((bm, d), lambda i: (i, 0)),
        out_shape=jax.ShapeDtypeStruct((m, d), out_dtype),
        compiler_params=_cparams(1),
        name=name,
    )(x, g.reshape(1, d))


def _rope(acc, cos, sin_lo, sin_hi):
    half = ROT_DIM // 2
    chunks = []
    for c in range(acc.shape[1] // LANES):
        x = acc[:, c * LANES:(c + 1) * LANES]
        up = pltpu.roll(x, LANES - half, 1)
        dn = pltpu.roll(x, half, 1)
        chunks.append(x * cos + up * sin_lo + dn * sin_hi)
    return jnp.concatenate(chunks, axis=1)


def _mm_kernel(x_ref, w_ref, *refs, epilogue, scale, rope, w_transposed, head_rows, aliased):
    if w_transposed:
        acc = lax.dot_general(x_ref[...], w_ref[...], _NT, preferred_element_type=jnp.float32)
    else:
        acc = jnp.dot(x_ref[...], w_ref[...], preferred_element_type=jnp.float32)
    if scale != 1.0:
        acc = acc * scale
    if rope:
        acc = _rope(acc, refs[0][...], refs[1][...], refs[2][...])
        refs = refs[3:]
    if aliased:
        refs = refs[1:]
    if epilogue == "plain":
        for o_ref in refs:
            o_ref[...] = acc.astype(o_ref.dtype)
    elif epilogue == "heads":
        hm_ref, o_ref = refs
        bm = acc.shape[0]
        for j, r in enumerate(head_rows):
            hm_ref[pl.ds(r, bm, stride=GROUPS), :] = acc[:, j * LANES:(j + 1) * LANES]
        o_ref[...] = acc.astype(o_ref.dtype)
    elif epilogue == "sigmoid":
        (o_ref,) = refs
        o_ref[...] = jax.nn.sigmoid(acc).astype(o_ref.dtype)
    elif epilogue == "log_sigmoid":
        b_ref, o_ref = refs
        o_ref[...] = jax.nn.log_sigmoid(acc + b_ref[...]).astype(o_ref.dtype)
    elif epilogue == "relu2":
        (o_ref,) = refs
        r = jnp.maximum(acc, 0.0)
        o_ref[...] = (r * r).astype(o_ref.dtype)
    elif epilogue == "residual":
        res_ref, o_ref = refs
        o_ref[...] = res_ref[...] + acc
    elif epilogue == "residual_norm":
        res_ref, g_ref, x_ref_out, h_ref_out = refs
        xn = res_ref[...] + acc
        x_ref_out[...] = xn
        ms = jnp.mean(xn * xn, axis=-1, keepdims=True)
        h_ref_out[...] = (xn * lax.rsqrt(ms + NORM_EPS) * g_ref[...]).astype(h_ref_out.dtype)
    else:
        raise ValueError(epilogue)


def matmul(x, w, *, epilogue, out_dtypes, name, bm, bn, scale=1.0, extras=(), rope_tables=None,
           w_transposed=False, head_rows=None, stack=None):
    m, k = x.shape
    n = w.shape[0] if w_transposed else w.shape[1]
    bm, bn = min(bm, m), min(bn, n)
    assert m % bm == 0 and n % bn == 0
    in_specs = [pl.BlockSpec((bm, k), lambda i, j: (i, 0)),
                pl.BlockSpec((bn, k), lambda i, j: (j, 0)) if w_transposed
                else pl.BlockSpec((k, bn), lambda i, j: (0, j))]
    args = [x, w]
    if rope_tables is not None:
        tables, n_blocks = rope_tables
        for t in tables:
            in_specs.append(pl.BlockSpec((bm, LANES), lambda i, j: (i % n_blocks, 0)))
            args.append(t)
    for arr, blk, imap in extras:
        in_specs.append(pl.BlockSpec(blk, imap))
        args.append(arr)
    aliases = {}
    if epilogue == "heads":
        assert bn == GROUPS * LANES == n
        slot, n_slots, previous = stack if stack is not None else (0, 1, None)
        first_block = slot * (m // bm)
        if previous is not None:
            in_specs.append(pl.BlockSpec(memory_space=pl.ANY))
            args.append(previous)
            aliases = {len(args) - 1: 0}
        out_specs = [pl.BlockSpec((bm * GROUPS, LANES), lambda i, j: (first_block + i, 0)),
                     pl.BlockSpec((bm, bn), lambda i, j: (i, j))]
        out_shape = [jax.ShapeDtypeStruct((n_slots * m * GROUPS, LANES), out_dtypes[0]),
                     jax.ShapeDtypeStruct((m, n), out_dtypes[1])]
    else:
        out_specs = [pl.BlockSpec((bm, bn), lambda i, j: (i, j)) for _ in out_dtypes]
        out_shape = [jax.ShapeDtypeStruct((m, n), dt) for dt in out_dtypes]
    outs = pl.pallas_call(
        functools.partial(_mm_kernel, epilogue=epilogue, scale=scale, rope=rope_tables is not None,
                          w_transposed=w_transposed, head_rows=head_rows, aliased=bool(aliases)),
        grid=(m // bm, n // bn),
        in_specs=in_specs, out_specs=out_specs, out_shape=out_shape,
        input_output_aliases=aliases,
        compiler_params=_cparams(2),
        name=name,
    )(*args)
    return outs[0] if len(outs) == 1 else outs


def _mix_kernel(h_ref, fo_ref, do_ref, wf_ref, wd_ref, wg0_ref, wg1_ref, o_ref):
    h = h_ref[...]
    g0 = jax.nn.sigmoid(lax.dot_general(h, wg0_ref[...], _NT, preferred_element_type=jnp.float32))
    g1 = jax.nn.sigmoid(lax.dot_general(h, wg1_ref[...], _NT, preferred_element_type=jnp.float32))
    a = jnp.dot(fo_ref[...], wf_ref[...], preferred_element_type=jnp.float32)
    b = jnp.dot(do_ref[...], wd_ref[...], preferred_element_type=jnp.float32)
    o_ref[...] = (g0 * a + g1 * b).astype(o_ref.dtype)


def gated_mix(h, fo, do, w_fox_out, w_diff_out, w_gates_t, name):
    m, kf = fo.shape
    kd = do.shape[1]
    kh = h.shape[1]
    d = w_fox_out.shape[1]
    bm, bn = min(1024, m), 512
    nj = d // bn
    return pl.pallas_call(
        _mix_kernel,
        grid=(m // bm, nj),
        in_specs=[pl.BlockSpec((bm, kh), lambda i, j: (i, 0)),
                  pl.BlockSpec((bm, kf), lambda i, j: (i, 0)),
                  pl.BlockSpec((bm, kd), lambda i, j: (i, 0)),
                  pl.BlockSpec((kf, bn), lambda i, j: (0, j)),
                  pl.BlockSpec((kd, bn), lambda i, j: (0, j)),
                  pl.BlockSpec((bn, kh), lambda i, j: (j, 0)),
                  pl.BlockSpec((bn, kh), lambda i, j: (j + nj, 0))],
        out_specs=pl.BlockSpec((bm, bn), lambda i, j: (i, j)),
        out_shape=jax.ShapeDtypeStruct((m, d), jnp.bfloat16),
        compiler_params=_cparams(2),
        name=name,
    )(h, fo, do, w_fox_out, w_diff_out, w_gates_t, w_gates_t)


def _lane_prefix(x):
    lane = lax.broadcasted_iota(jnp.int32, x.shape, 1)
    shift = 1
    while shift < LANES:
        x = x + jnp.where(lane >= shift, pltpu.roll(x, shift, 1), 0.0)
        shift *= 2
    return x


def _fox_prep_kernel(lf_ref, ft_ref, fn_ref, *, n_heads):
    seq = lf_ref.shape[1]
    lt = lf_ref[0].T[:n_heads, :]
    blocks = [_lane_prefix(lt[:, c * LANES:(c + 1) * LANES]) for c in range(seq // LANES)]
    carry = jnp.zeros((n_heads, 1), jnp.float32)
    outs = []
    for blk in blocks:
        blk = blk + carry
        carry = blk[:, LANES - 1:LANES]
        outs.append(blk)
    ft = jnp.concatenate(outs, axis=1)
    ft_ref[0] = ft
    padded = jnp.concatenate([ft, jnp.zeros((LANES - n_heads, seq), jnp.float32)], axis=0)
    fn_ref[0] = padded.T


def fox_prep(logf_pad, n_heads, name):
    b, seq, _ = logf_pad.shape
    return pl.pallas_call(
        functools.partial(_fox_prep_kernel, n_heads=n_heads),
        grid=(b,),
        in_specs=[pl.BlockSpec((1, seq, LANES), lambda i: (i, 0, 0))],
        out_specs=[pl.BlockSpec((1, n_heads, seq), lambda i: (i, 0, 0)),
                   pl.BlockSpec((1, seq, LANES), lambda i: (i, 0, 0))],
        out_shape=[jax.ShapeDtypeStruct((b, n_heads, seq), jnp.float32),
                   jax.ShapeDtypeStruct((b, seq, LANES), jnp.float32)],
        compiler_params=_cparams(1),
        name=name,
    )(logf_pad)


ROW_CHUNK = 64
COL_CHUNK = 256


def _softmax_block(s_ref, p_ref, m_ref, l_ref, acc_ref, v, key_term=None, row_shift=None, lead=None):
    tq, tk = s_ref.shape
    col_minus_row = (lax.broadcasted_iota(jnp.int32, (ROW_CHUNK, COL_CHUNK), 1)
                     - lax.broadcasted_iota(jnp.int32, (ROW_CHUNK, COL_CHUNK), 0))
    for r in range(tq // ROW_CHUNK):
        r0 = r * ROW_CHUNK
        rs = slice(r0, r0 + ROW_CHUNK)

        def piece(c, r0=r0, rs=rs):
            c0 = c * COL_CHUNK
            t = s_ref[rs, c0:c0 + COL_CHUNK]
            if key_term is not None:
                t = t - key_term[:, c0:c0 + COL_CHUNK]
            if lead is not None:
                t = jnp.where(col_minus_row <= lead - (c0 - r0), t, -jnp.inf)
            return t

        n_col = tk // COL_CHUNK
        mx = piece(0)
        for c in range(1, n_col):
            mx = jnp.maximum(mx, piece(c))
        t_max = jnp.max(mx, axis=-1, keepdims=True)
        m_prev = m_ref[rs, :]
        if row_shift is not None:
            shift_r = row_shift[rs, :]
            m_new = jnp.maximum(m_prev, t_max + shift_r)
            sub = m_new - shift_r
        else:
            m_new = jnp.maximum(m_prev, t_max)
            sub = m_new
        alpha = jnp.exp(m_prev - m_new)
        lsum = None
        for c in range(n_col):
            p = jnp.exp(piece(c) - sub)
            lsum = p if lsum is None else lsum + p
            p_ref[rs, c * COL_CHUNK:(c + 1) * COL_CHUNK] = p.astype(p_ref.dtype)
        l_ref[rs, :] = alpha * l_ref[rs, :] + jnp.sum(lsum, axis=-1, keepdims=True)
        acc_ref[rs, :] = alpha * acc_ref[rs, :]
        m_ref[rs, :] = m_new
    acc_ref[...] += jnp.dot(p_ref[...], v, preferred_element_type=jnp.float32)


def _causal_blocks(qi, tq, tk):
    assert tk % tq == 0
    n_full = qi // (tk // tq)
    return n_full, qi * tq - n_full * tk


def _fox_attn_kernel(q_ref, k_ref, v_ref, ft_ref, fn_ref, o_ref, m_ref, l_ref, acc_ref, s_ref, p_ref,
                     *, tq, tk, n_heads):
    n_full, last_lead = _causal_blocks(pl.program_id(1), tq, tk)
    m_ref[...] = jnp.full(m_ref.shape, -jnp.inf, jnp.float32)
    l_ref[...] = jnp.zeros(l_ref.shape, jnp.float32)
    acc_ref[...] = jnp.zeros(acc_ref.shape, jnp.float32)

    def block(start, lead):
        for h in range(n_heads):
            sl = slice(h * HEAD_DIM, (h + 1) * HEAD_DIM)
            k = k_ref[0, pl.ds(start, tk), sl]
            s_ref[h] = lax.dot_general(q_ref[0, :, sl], k, _NT, preferred_element_type=jnp.float32)
            fk = ft_ref[0, h:h + 1, pl.ds(start, tk)]
            fq = fn_ref[0, :, h:h + 1]
            _softmax_block(s_ref.at[h], p_ref.at[h], m_ref.at[h], l_ref.at[h], acc_ref.at[h],
                           v_ref[0, pl.ds(start, tk), sl], key_term=fk, row_shift=fq, lead=lead)

    def body(kj, carry):
        block(pl.multiple_of(kj * tk, tk), None)
        return carry

    lax.fori_loop(0, n_full, body, 0)
    block(pl.multiple_of(n_full * tk, tk), last_lead)
    for h in range(n_heads):
        o_ref[0, :, h * HEAD_DIM:(h + 1) * HEAD_DIM] = (acc_ref[h] / l_ref[h]).astype(o_ref.dtype)


def fox_attention(q, k, v, ft, fnat, n_heads, name, tq=256, tk=1024):
    b, seq, width = q.shape
    return pl.pallas_call(
        functools.partial(_fox_attn_kernel, tq=tq, tk=tk, n_heads=n_heads),
        grid=(b, seq // tq),
        in_specs=[pl.BlockSpec((1, tq, width), lambda i, j: (i, j, 0)),
                  pl.BlockSpec((1, seq, width), lambda i, j: (i, 0, 0)),
                  pl.BlockSpec((1, seq, width), lambda i, j: (i, 0, 0)),
                  pl.BlockSpec((1, n_heads, seq), lambda i, j: (i, 0, 0)),
                  pl.BlockSpec((1, tq, LANES), lambda i, j: (i, j, 0))],
        out_specs=pl.BlockSpec((1, tq, width), lambda i, j: (i, j, 0)),
        out_shape=jax.ShapeDtypeStruct((b, seq, width), jnp.bfloat16),
        scratch_shapes=[pltpu.VMEM((n_heads, tq, 1), jnp.float32),
                        pltpu.VMEM((n_heads, tq, 1), jnp.float32),
                        pltpu.VMEM((n_heads, tq, HEAD_DIM), jnp.float32),
                        pltpu.VMEM((n_heads, tq, tk), jnp.float32),
                        pltpu.VMEM((n_heads, tq, tk), jnp.bfloat16)],
        compiler_params=_cparams(2),
        name=name,
    )(q, k, v, ft, fnat)


def _lambda(lq1_ref, lk1_ref, lq2_ref, lk2_ref, lam_init):
    d1 = jnp.sum(lq1_ref[...] * lk1_ref[...], axis=-1, keepdims=True)
    d2 = jnp.sum(lq2_ref[...] * lk2_ref[...], axis=-1, keepdims=True)
    return jnp.exp(d1) - jnp.exp(d2) + lam_init


def _subln(o, g, lam_init):
    ms = jnp.mean(o * o, axis=-1, keepdims=True)
    return o * lax.rsqrt(ms + SUBLN_EPS) * g * (1.0 - lam_init)


def _diff_attn_kernel(q_ref, k_ref, v_ref, lq1_ref, lk1_ref, lq2_ref, lk2_ref, g_ref, o_ref,
                      m_ref, l_ref, acc_ref, s_ref, p_ref, *, tq, tk, n_heads, lam_init):
    n_full, last_lead = _causal_blocks(pl.program_id(1), tq, tk)
    lam = _lambda(lq1_ref, lk1_ref, lq2_ref, lk2_ref, lam_init)
    dv = 2 * HEAD_DIM
    m_ref[...] = jnp.full(m_ref.shape, -jnp.inf, jnp.float32)
    l_ref[...] = jnp.zeros(l_ref.shape, jnp.float32)
    acc_ref[...] = jnp.zeros(acc_ref.shape, jnp.float32)

    def block(start, lead):
        for g in range(2 * n_heads):
            sl = slice(g * HEAD_DIM, (g + 1) * HEAD_DIM)
            k = k_ref[0, pl.ds(start, tk), sl]
            s_ref[g] = lax.dot_general(q_ref[0, :, sl], k, _NT, preferred_element_type=jnp.float32)
            v = v_ref[0, pl.ds(start, tk), (g // 2) * dv:(g // 2 + 1) * dv]
            _softmax_block(s_ref.at[g], p_ref.at[g], m_ref.at[g], l_ref.at[g], acc_ref.at[g], v, lead=lead)

    def body(kj, carry):
        block(pl.multiple_of(kj * tk, tk), None)
        return carry

    lax.fori_loop(0, n_full, body, 0)
    block(pl.multiple_of(n_full * tk, tk), last_lead)
    for h in range(n_heads):
        o = acc_ref[2 * h] / l_ref[2 * h] - lam * (acc_ref[2 * h + 1] / l_ref[2 * h + 1])
        o_ref[0, :, h * dv:(h + 1) * dv] = _subln(o, g_ref[...], lam_init).astype(o_ref.dtype)


def diff_attention(q, k, v, lam_vecs, g_subln, lam_init, n_heads, name, tq=256, tk=1024):
    b, seq, width = q.shape
    vec = pl.BlockSpec((1, HEAD_DIM), lambda i, j: (0, 0))
    return pl.pallas_call(
        functools.partial(_diff_attn_kernel, tq=tq, tk=tk, n_heads=n_heads, lam_init=lam_init),
        grid=(b, seq // tq),
        in_specs=[pl.BlockSpec((1, tq, width), lambda i, j: (i, j, 0)),
                  pl.BlockSpec((1, seq, width), lambda i, j: (i, 0, 0)),
                  pl.BlockSpec((1, seq, width), lambda i, j: (i, 0, 0)),
                  vec, vec, vec, vec,
                  pl.BlockSpec((1, 2 * HEAD_DIM), lambda i, j: (0, 0))],
        out_specs=pl.BlockSpec((1, tq, width), lambda i, j: (i, j, 0)),
        out_shape=jax.ShapeDtypeStruct((b, seq, width), jnp.bfloat16),
        scratch_shapes=[pltpu.VMEM((2 * n_heads, tq, 1), jnp.float32),
                        pltpu.VMEM((2 * n_heads, tq, 1), jnp.float32),
                        pltpu.VMEM((2 * n_heads, tq, 2 * HEAD_DIM), jnp.float32),
                        pltpu.VMEM((2 * n_heads, tq, tk), jnp.float32),
                        pltpu.VMEM((2 * n_heads, tq, tk), jnp.bfloat16)],
        compiler_params=_cparams(2),
        name=name,
    )(q, k, v, *lam_vecs, g_subln)


def _decode_kernel(pt_ref, qf_ref, qd_ref, kfn_ref, vfn_ref, kdn_ref, vdn_ref, lfn_ref,
                   lq1_ref, lk1_ref, lq2_ref, lk2_ref, g_ref, *rest,
                   n_tok, pages_per_step, lam_init):
    n_cache = 5 * pages_per_step
    cache = rest[:n_cache]
    fo_ref, do_ref = rest[n_cache:n_cache + 2]
    (new_ref, crow_ref, off_ref, mf_ref, lf_ref, accf_ref, md_ref, ld_ref, accd_ref) = rest[n_cache + 2:]
    step = pl.program_id(1)
    page = LANES
    n_diff = GROUPS // 2
    bf = jnp.bfloat16
    row = lax.broadcasted_iota(jnp.int32, (Q_ROWS, page), 0)
    lane = lax.broadcasted_iota(jnp.int32, (Q_ROWS, page), 1)

    def group_rows(x):
        return jnp.concatenate([jnp.broadcast_to(x[g:g + 1, :], (Q_ROWS, x.shape[1]))
                                for g in range(GROUPS)], axis=0)

    def logits(q_ref, key_slabs):
        out = []
        for g in range(GROUPS):
            k = jnp.concatenate(key_slabs[g], axis=0).astype(bf)
            out.append(lax.dot_general(q_ref[0, g], k, _NT, preferred_element_type=jnp.float32))
        return jnp.concatenate(out, axis=0)

    def softmax_update(s, m_r, l_r):
        m_prev = m_r[...]
        m_new = jnp.maximum(m_prev, jnp.max(s, axis=-1, keepdims=True))
        alpha = jnp.exp(m_prev - m_new)
        p = jnp.exp(s - m_new)
        l_r[...] = alpha * l_r[...] + jnp.sum(p, axis=-1, keepdims=True)
        m_r[...] = m_new
        return alpha, p.astype(bf)

    def visit(slab, n_visit, key_term, mask):
        pages = range(n_visit)
        s = logits(qf_ref, [[slab(p, 0, g) for p in pages] for g in range(GROUPS)])
        s = s + (crow_ref[...] + group_rows(key_term))
        if mask is not None:
            s = jnp.where(mask, s, -jnp.inf)
        alpha, p_bf = softmax_update(s, mf_ref, lf_ref)
        pv = []
        for g in range(GROUPS):
            v = jnp.concatenate([slab(p, 1, g) for p in pages], axis=0).astype(bf)
            pv.append(jnp.dot(p_bf[g * Q_ROWS:(g + 1) * Q_ROWS], v, preferred_element_type=jnp.float32))
        accf_ref[...] = alpha * accf_ref[...] + jnp.concatenate(pv, axis=0)
        s = logits(qd_ref, [[slab(p, 2, g) for p in pages] for g in range(GROUPS)])
        if mask is not None:
            s = jnp.where(mask, s, -jnp.inf)
        alpha, p_bf = softmax_update(s, md_ref, ld_ref)
        pv = []
        for h in range(n_diff):
            v = jnp.concatenate(
                [jnp.concatenate([slab(p, 3, h), slab(p, 3, n_diff + h)], axis=1) for p in pages],
                axis=0).astype(bf)
            pv.append(jnp.dot(p_bf[2 * h * Q_ROWS:(2 * h + 2) * Q_ROWS], v,
                              preferred_element_type=jnp.float32))
        accd_ref[...] = alpha * accd_ref[...] + jnp.concatenate(pv, axis=0)

    @pl.when(step == 0)
    def _first():
        for r in (mf_ref, md_ref):
            r[...] = jnp.full(r.shape, -jnp.inf, jnp.float32)
        for r in (lf_ref, ld_ref, accf_ref, accd_ref, off_ref):
            r[...] = jnp.zeros(r.shape, jnp.float32)
        new_ref[...] = jnp.zeros(new_ref.shape, jnp.float32)
        for i, src in enumerate((kfn_ref, vfn_ref, kdn_ref, vdn_ref)):
            new_ref[i, 0:n_tok * GROUPS, :] = src[0]
        pref = _lane_prefix(lfn_ref[0])
        diag = jnp.concatenate([lane == row] * GROUPS, axis=0)
        crow_ref[...] = jnp.sum(jnp.where(diag, group_rows(pref), 0.0), axis=-1, keepdims=True)
        causal = jnp.concatenate([lane <= row] * GROUPS, axis=0)
        visit(lambda p, i, g: new_ref[i, pl.ds(g, page, stride=GROUPS), :], 1, -pref, causal)

    key_terms = []
    off = off_ref[...]
    for slot in range(pages_per_step):
        pref = _lane_prefix(cache[5 * slot + 2][0, 0])
        total = pref[:, page - 1:page]
        key_terms.append(total - pref + off)
        off = off + total
    off_ref[...] = off
    kv_index = (0, 1, 3, 4)
    visit(lambda p, i, g: cache[5 * p + kv_index[i]][0, 0, pl.ds(g, page, stride=GROUPS), :],
          pages_per_step, jnp.concatenate(key_terms, axis=1), None)

    @pl.when(step == pl.num_programs(1) - 1)
    def _last():
        of = accf_ref[...] / lf_ref[...]
        for g in range(GROUPS):
            fo_ref[0, :, g * HEAD_DIM:(g + 1) * HEAD_DIM] = of[g * Q_ROWS:g * Q_ROWS + n_tok]
        lam = _lambda(lq1_ref, lk1_ref, lq2_ref, lk2_ref, lam_init)
        dv = 2 * HEAD_DIM
        od = accd_ref[...] / ld_ref[...]
        for h in range(n_diff):
            o = (od[2 * h * Q_ROWS:(2 * h + 1) * Q_ROWS]
                 - lam * od[(2 * h + 1) * Q_ROWS:(2 * h + 2) * Q_ROWS])
            do_ref[0, :, h * dv:(h + 1) * dv] = _subln(o, g_ref[...], lam_init)[0:n_tok]


def decode_attention(layer, page_table, q_fox, q_diff, new_kv, new_logf_t, caches, lam_vecs, g_subln,
                     lam_init, name):
    b, n_pages = page_table.shape
    n_tok = new_kv[0].shape[1] // GROUPS
    width = GROUPS * HEAD_DIM
    page = caches[0].shape[2] // GROUPS
    assert page == LANES
    pps = PAGES_PER_STEP
    assert n_pages % pps == 0

    def per_batch(shape):
        zeros = (0,) * len(shape)
        return pl.BlockSpec((1,) + shape, lambda i, s, pt: (i,) + zeros)

    def const(shape):
        return pl.BlockSpec(shape, lambda i, s, pt: (0, 0))

    def cache_spec(arr, slot):
        def imap(i, s, pt):
            return (layer, pt[i, n_pages - 1 - (s * pps + slot)], 0, 0)
        return pl.BlockSpec((1, 1) + arr.shape[2:], imap)

    in_specs = [per_batch((GROUPS, Q_ROWS, HEAD_DIM))] * 2
    in_specs += [per_batch((n_tok * GROUPS, LANES))] * 4
    in_specs += [per_batch((GROUPS, LANES))]
    in_specs += [const((1, HEAD_DIM))] * 4 + [const((1, 2 * HEAD_DIM))]
    args = [q_fox, q_diff, *new_kv, new_logf_t, *lam_vecs, g_subln]
    for slot in range(pps):
        for arr in caches:
            in_specs.append(cache_spec(arr, slot))
            args.append(arr)
    out_spec = pl.BlockSpec((1, n_tok, width), lambda i, s, pt: (i, 0, 0))
    rows = GROUPS * Q_ROWS
    stat = pltpu.VMEM((rows, 1), jnp.float32)
    grid_spec = pltpu.PrefetchScalarGridSpec(
        num_scalar_prefetch=1,
        grid=(b, n_pages // pps),
        in_specs=in_specs,
        out_specs=[out_spec, out_spec],
        scratch_shapes=[pltpu.VMEM((4, page * GROUPS, LANES), jnp.float32),
                        stat,
                        pltpu.VMEM((GROUPS, 1), jnp.float32),
                        stat, stat, pltpu.VMEM((rows, HEAD_DIM), jnp.float32),
                        stat, stat, pltpu.VMEM((rows, 2 * HEAD_DIM), jnp.float32)])
    out = jax.ShapeDtypeStruct((b, n_tok, width), jnp.float32)
    return pl.pallas_call(
        functools.partial(_decode_kernel, n_tok=n_tok, pages_per_step=pps, lam_init=lam_init),
        grid_spec=grid_spec,
        out_shape=[out, out],
        compiler_params=_cparams(2),
        name=name,
    )(page_table, *args)


def _rope_tables(pos):
    half = ROT_DIM // 2
    inv = ROPE_THETA ** (-jnp.arange(half, dtype=jnp.float32) / half)
    ang = pos.astype(jnp.float32)[:, None] * inv[None, :]
    cos, sin = jnp.cos(ang), jnp.sin(ang)
    n = pos.shape[0]
    one = jnp.ones((n, LANES - ROT_DIM), jnp.float32)
    zero_half = jnp.zeros((n, half), jnp.float32)
    zero_rest = jnp.zeros((n, LANES - ROT_DIM), jnp.float32)
    return (jnp.concatenate([cos, cos, one], axis=1),
            jnp.concatenate([-sin, zero_half, zero_rest], axis=1),
            jnp.concatenate([zero_half, sin, zero_rest], axis=1))


def _project(h, w, rope, tag, bm, stacks=None):
    bf, f32 = jnp.bfloat16, jnp.float32
    stacks = stacks or {}
    mm = functools.partial(matmul, h, bm=bm, bn=1024, w_transposed=True)
    fq = mm(w["q"], epilogue="plain", out_dtypes=[bf], scale=ATTN_SCALE, name=f"{tag}_fq")
    dq = mm(w["dq"], epilogue="plain", out_dtypes=[bf], scale=ATTN_SCALE, rope_tables=rope, name=f"{tag}_dq")
    heads = functools.partial(mm, epilogue="heads", out_dtypes=[f32, bf])
    fk = heads(w["k"], head_rows=_ROWS_IDENTITY, stack=stacks.get("k"), name=f"{tag}_fk")
    fv = heads(w["v"], head_rows=_ROWS_IDENTITY, stack=stacks.get("v"), name=f"{tag}_fv")
    dk = heads(w["dk"], head_rows=_ROWS_IDENTITY, rope_tables=rope, stack=stacks.get("dk"), name=f"{tag}_dk")
    dv = heads(w["dv"], head_rows=_ROWS_DIFF_V, stack=stacks.get("dv"), name=f"{tag}_dv")
    gates = h
    logf = mm(w["f"], epilogue="log_sigmoid", out_dtypes=[f32],
              extras=[(w["bf"], (1, LANES), lambda i, j: (0, 0))], name=f"{tag}_logf")
    return fq, dq, fk, fv, dk, dv, gates, logf


def _finish(x, fo, do, gates, w, tag):
    m, d = x.shape
    mix = gated_mix(gates, fo, do, w["fox_out"], w["diff_out"], w["g"], name=f"{tag}_mix")
    bm = min(512, m)
    x2, h2 = matmul(mix, w["o"], epilogue="residual_norm", out_dtypes=[jnp.float32, jnp.bfloat16],
                    bm=bm, bn=d, name=f"{tag}_wo",
                    extras=[(x, (bm, d), lambda i, j: (i, 0)),
                            (w["g_mlp"], (1, d), lambda i, j: (0, 0))])
    u = matmul(h2, w["up"], epilogue="relu2", out_dtypes=[jnp.bfloat16], bm=1024, bn=1024, name=f"{tag}_up")
    return matmul(u, w["down"], epilogue="residual", out_dtypes=[jnp.float32], bm=bm, bn=512,
                  name=f"{tag}_down", extras=[(x2, (bm, 512), lambda i, j: (i, j))])


def kernel(x_prompt, x_sample, cache_fox_k, cache_fox_v, cache_fox_logf, cache_diff_k, cache_diff_v,
           page_table, g_attn, w_in, b_f, lam_q1, lam_k1, lam_q2, lam_k2, g_subln,
           w_fox_out, w_diff_out, w_o, g_mlp, w_up, w_down, g_final):
    bp, seq, d = x_prompt.shape
    bs, n_tok, _ = x_sample.shape
    depth, n_pool, page, n_fox, _ = cache_fox_k.shape
    n_diff = cache_diff_k.shape[3]
    assert n_fox == GROUPS and 2 * n_diff == GROUPS
    fox_w = n_fox * HEAD_DIM
    diff_w = n_diff * 2 * HEAD_DIM
    n_pages = page_table.shape[1]
    past_len = n_pages * page
    bf = jnp.bfloat16

    flat = (depth, n_pool, page * GROUPS, HEAD_DIM)
    dv_cache = cache_diff_v.reshape(depth, n_pool, page, n_diff, 2, HEAD_DIM)
    caches = (cache_fox_k.reshape(flat), cache_fox_v.reshape(flat),
              jnp.swapaxes(cache_fox_logf, 2, 3),
              cache_diff_k.reshape(flat),
              jnp.swapaxes(dv_cache, 3, 4).reshape(flat))

    bm_p = 1024
    rope_p = (_rope_tables(jnp.arange(seq)), seq // bm_p)
    rope_s = (_rope_tables(past_len + jnp.arange(bs * n_tok) % n_tok), 1)

    xp = x_prompt.reshape(bp * seq, d)
    xs = x_sample.reshape(bs * n_tok, d)
    w_in_t = jnp.swapaxes(w_in, 1, 2)
    stacked_p = {}
    logf_p = []
    outs_s = [[] for _ in range(5)]

    for l in range(depth):
        lam_init = 0.8 - 0.6 * math.exp(-0.3 * l)
        o = 0
        parts = {}
        for key, size in (("q", fox_w), ("k", fox_w), ("v", fox_w), ("f", n_fox),
                          ("dq", diff_w), ("dk", diff_w), ("dv", diff_w), ("g", 2 * d)):
            parts[key] = w_in_t[l, o:o + size].astype(bf)
            o += size
        parts["f"] = jnp.pad(parts["f"], ((0, LANES - n_fox), (0, 0)))
        parts["bf"] = jnp.pad(b_f[l], (0, LANES - n_fox)).reshape(1, LANES)
        parts.update(fox_out=w_fox_out[l].astype(bf), diff_out=w_diff_out[l].astype(bf),
                     o=w_o[l].astype(bf), up=w_up[l].astype(bf), down=w_down[l].astype(bf),
                     g_mlp=g_mlp[l].reshape(1, d))
        lam_vecs = [v[l].reshape(1, HEAD_DIM) for v in (lam_q1, lam_k1, lam_q2, lam_k2)]
        gs = g_subln[l].reshape(1, 2 * HEAD_DIM)

        hp = rmsnorm(xp, g_attn[l], NORM_EPS, bf, name=f"p{l}_norm")
        stacks = {key: (l, depth, stacked_p.get(key)) for key in ("k", "v", "dk", "dv")}
        fq, dq, (fk, fkb), (fv, fvb), (dk, dkb), (dv, dvb), gates, logf = _project(
            hp, parts, rope_p, f"p{l}", bm_p, stacks)
        stacked_p.update(k=fk, v=fv, dk=dk, dv=dv)
        ft, fnat = fox_prep(logf.reshape(bp, seq, LANES), n_fox, name=f"p{l}_foxprep")
        shp = lambda a: a.reshape(bp, seq, -1)
        fo = fox_attention(shp(fq), shp(fkb), shp(fvb), ft, fnat, n_fox, name=f"p{l}_fox")
        do = diff_attention(shp(dq), shp(dkb), shp(dvb), lam_vecs, gs, lam_init, n_diff, name=f"p{l}_diff")
        xp = _finish(xp, fo.reshape(bp * seq, fox_w), do.reshape(bp * seq, diff_w), gates, parts, f"p{l}")
        logf_p.append(logf[:, :n_fox])

        hs = rmsnorm(xs, g_attn[l], NORM_EPS, bf, name=f"s{l}_norm")
        fq, dq, (fk, _), (fv, _), (dk, _), (dv, _), gates, logf = _project(hs, parts, rope_s, f"s{l}", 512)

        def group_queries(q):
            q = jnp.swapaxes(q.reshape(bs, n_tok, GROUPS, HEAD_DIM), 1, 2)
            return jnp.pad(q, ((0, 0), (0, 0), (0, Q_ROWS - n_tok), (0, 0)))

        new_kv = [a.reshape(bs, n_tok * GROUPS, HEAD_DIM) for a in (fk, fv, dk, dv)]
        lf_t = jnp.swapaxes(logf[:, :n_fox].reshape(bs, n_tok, n_fox), 1, 2)
        lf_t = jnp.pad(lf_t, ((0, 0), (0, 0), (0, LANES - n_tok)))
        fo, do = decode_attention(l, page_table, group_queries(fq), group_queries(dq), new_kv, lf_t,
                                  caches, lam_vecs, gs, lam_init, name=f"s{l}_decode")
        xs = _finish(xs, fo.reshape(bs * n_tok, fox_w).astype(bf), do.reshape(bs * n_tok, diff_w).astype(bf),
                     gates, parts, f"s{l}")
        for lst, a in zip(outs_s, (fk, fv, logf[:, :n_fox], dk, dv)):
            lst.append(a)

    y_p = rmsnorm(xp, g_final, NORM_EPS, jnp.float32, name="p_final_norm").reshape(bp, seq, d)
    y_s = rmsnorm(xs, g_final, NORM_EPS, jnp.float32, name="s_final_norm").reshape(bs, n_tok, d)

    def assemble(arrays, lead):
        fk, fv, lf, dk, dv = arrays
        dv = jnp.swapaxes(dv.reshape((depth,) + lead + (2, n_diff, HEAD_DIM)), -3, -2)
        return (fk.reshape((depth,) + lead + (n_fox, HEAD_DIM)),
                fv.reshape((depth,) + lead + (n_fox, HEAD_DIM)),
                lf.reshape((depth,) + lead + (n_fox,)),
                dk.reshape((depth,) + lead + (n_diff, 2, HEAD_DIM)),
                dv.reshape((depth,) + lead + (n_diff, 2 * HEAD_DIM)))

    prompt = (stacked_p["k"], stacked_p["v"], jnp.stack(logf_p), stacked_p["dk"], stacked_p["dv"])
    sample = [jnp.stack(lst) for lst in outs_s]
    return (y_p, y_s, *assemble(prompt, (bp, seq)), *assemble(sample, (bs, n_tok)))
```
